```python
import math
import jax, jax.numpy as jnp
from jax import lax
import numpy as np

D_MODEL = 1024
BATCH = 8
SEQ = 4096
DEPTH = 1

GRID_W = 64
CTX_LEN = 256
HEAD_DIM = 64
NA_HEADS = 8
NA_WIDTH = NA_HEADS * HEAD_DIM
NA_WIN_H = 8
NA_WIN_W = 16
HY_WIDTH = D_MODEL // 2
HY_ORDER = 2
HY_SHORT = 3
HY_EMB = 33
HY_BANDS = (HY_EMB - 1) // 2
HY_FFN = 64
HY_FAST_DECAY = 0.3
HY_SLOW_DECAY = 1.5
HY_TARGET = 1e-2
HY_FILTER_SCALE = 0.008
HY_PROJ = (HY_ORDER + 1) * HY_WIDTH
FFN_HIDDEN = -(-(8 * D_MODEL) // (3 * 256)) * 256
N_MOD = 6
ROPE_THETA = 10000.0
RMS_EPS = 1e-6
NEG_INF = -1e30
IN_SPLITS = (NA_WIDTH, 2 * NA_WIDTH, 3 * NA_WIDTH, 3 * NA_WIDTH + HY_WIDTH,
             3 * NA_WIDTH + 2 * HY_WIDTH, 3 * NA_WIDTH + HY_PROJ, 3 * NA_WIDTH + HY_PROJ + D_MODEL)
IN_WIDTH = 3 * NA_WIDTH + HY_PROJ + 2 * D_MODEL

kernel_name = 'hybrid_na_hyena_dit_block'


def rmsnorm(x, g):
    xf = x.astype(jnp.float32)
    y = xf * lax.rsqrt(jnp.mean(xf * xf, axis=-1, keepdims=True) + RMS_EPS)
    return (y * g.astype(jnp.float32)).astype(x.dtype)


def modulate(h, shift, scale):
    return h * (1 + scale) + shift


def to_heads(a):
    B, L, _ = a.shape
    return a.reshape(B, L, NA_HEADS, HEAD_DIM)


def axial_rope(x, rows, cols):
    nf = HEAD_DIM // 4
    inv = ROPE_THETA ** (-jnp.arange(nf, dtype=jnp.float32) / nf)

    def rot(xp, pos):
        ang = pos.astype(jnp.float32)[:, None] * inv
        cos = jnp.cos(ang)[None, :, None, :]
        sin = jnp.sin(ang)[None, :, None, :]
        x1 = xp[..., :nf].astype(jnp.float32)
        x2 = xp[..., nf:].astype(jnp.float32)
        return jnp.concatenate([x1 * cos - x2 * sin, x2 * cos + x1 * sin], axis=-1)

    half = HEAD_DIM // 2
    return jnp.concatenate([rot(x[..., :half], rows), rot(x[..., half:], cols)], axis=-1).astype(x.dtype)


def neighbourhood_attention(q, k, v, k_ctx, v_ctx, rpb):
    B, L, H, Dh = q.shape
    R = L // GRID_W
    wh = min(NA_WIN_H, R)
    scale = Dh ** -0.5
    t = jnp.arange(L)
    rows, cols = t // GRID_W, t % GRID_W
    q_rot = axial_rope(q, rows, cols)
    k_rot = axial_rope(k, rows, cols)

    def grid(a):
        return a.reshape(B, R, GRID_W, H, Dh)

    r = jnp.arange(R)
    row_idx = jnp.clip(r - wh // 2, 0, R - wh)[:, None] + jnp.arange(wh)[None, :]
    k_band = grid(k_rot)[:, row_idx]
    v_band = grid(v)[:, row_idx]
    cq = jnp.arange(GRID_W)
    start_c = jnp.clip(cq - NA_WIN_W // 2, 0, GRID_W - NA_WIN_W)
    col_ok = (cq[None, :] >= start_c[:, None]) & (cq[None, :] < start_c[:, None] + NA_WIN_W)
    roff = row_idx - r[:, None] + (NA_WIN_H - 1)
    coff = jnp.clip(cq[None, :] - cq[:, None], -(NA_WIN_W - 1), NA_WIN_W - 1) + (NA_WIN_W - 1)
    bias = rpb.astype(jnp.float32)[:, roff[:, None, :, None], coff[None, :, None, :]]
    s_nb = jnp.einsum('brqhd,brikhd->bhrqik', grid(q_rot), k_band,
                      preferred_element_type=jnp.float32) * scale + bias[None]
    s_nb = jnp.where(col_ok[:, None, :], s_nb, NEG_INF)
    s_ctx = jnp.einsum('brqhd,bchd->bhrqc', grid(q), k_ctx, preferred_element_type=jnp.float32) * scale
    s = jnp.concatenate([s_nb.reshape(B, H, R, GRID_W, wh * GRID_W), s_ctx], axis=-1)
    p = jax.nn.softmax(s, axis=-1).astype(v.dtype)
    p_nb = p[..., :wh * GRID_W].reshape(B, H, R, GRID_W, wh, GRID_W)
    p_ctx = p[..., wh * GRID_W:]
    o = (jnp.einsum('bhrqik,brikhd->brqhd', p_nb, v_band)
         + jnp.einsum('bhrqc,bchd->brqhd', p_ctx, v_ctx))
    return o.reshape(B, L, H * Dh)


def context_attention(q, k, v):
    B, Lc, H, Dh = q.shape
    s = jnp.einsum('bqhd,bkhd->bhqk', q, k, preferred_element_type=jnp.float32) * (Dh ** -0.5)
    p = jax.nn.softmax(s, axis=-1).astype(v.dtype)
    return jnp.einsum('bhqk,bkhd->bqhd', p, v).reshape(B, Lc, H * Dh)


def short_conv(u, w, b):
    up = jnp.pad(u, ((0, 0), (1, 1), (0, 0)))
    return up[:, :-2] * w[0] + up[:, 1:-1] * w[1] + up[:, 2:] * w[2] + b


def hyena_filters(L, w1, b1, w2, b2, freq, w3):
    f32 = jnp.float32
    t = jnp.linspace(0.0, 1.0, L, dtype=f32)[:, None]
    w = 2.0 * math.pi * jnp.arange(L, dtype=f32)[:, None] / L
    bands = jnp.linspace(1e-4, HY_BANDS - 1, HY_BANDS, dtype=f32)
    z = jnp.concatenate([t, jnp.cos(bands * w), jnp.sin(-bands * w)], axis=-1)
    fr = freq.astype(f32)
    hid = jnp.sin(fr * (z @ w1.astype(f32) + b1.astype(f32)))
    hid = jnp.sin(fr * (hid @ w2.astype(f32) + b2.astype(f32)))
    h = (hid @ w3.astype(f32)).reshape(L, 2, HY_ORDER, HY_WIDTH)
    min_decay = math.log(HY_TARGET) / HY_SLOW_DECAY
    max_decay = math.log(HY_TARGET) / HY_FAST_DECAY
    deltas = jnp.abs(jnp.linspace(min_decay, max_decay, HY_WIDTH, dtype=f32))
    decay = jnp.exp(-t * deltas)
    return h * decay[:, None, None, :]


def long_conv_bidir(u, h_fwd, h_bwd, bias):
    B, L, C = u.shape
    k2 = jnp.concatenate([h_fwd, jnp.zeros((1, C), jnp.float32), h_bwd[:0:-1]], axis=0)
    U = jnp.fft.rfft(u.astype(jnp.float32), n=2 * L, axis=1)
    K = jnp.fft.rfft(k2, n=2 * L, axis=0)
    y = jnp.fft.irfft(U * K[None], n=2 * L, axis=1)[:, :L]
    return (y + u.astype(jnp.float32) * bias.astype(jnp.float32)).astype(u.dtype)


def hyena_mixer(hv, hx1, hx2, conv_w, conv_b, w1, b1, w2, b2, freq, w3, bias):
    u = short_conv(jnp.concatenate([hv, hx1, hx2], axis=-1), conv_w, conv_b)
    parts = jnp.split(u, HY_ORDER + 1, axis=-1)
    z, gates = parts[0], parts[1:]
    filt = hyena_filters(z.shape[1], w1, b1, w2, b2, freq, w3)
    for n in range(HY_ORDER):
        z = gates[n] * long_conv_bidir(z, filt[:, 0, n], filt[:, 1, n], bias[n])
    return z


def gated_merge(g_na, g_hy, y_na, y_hy, w_na_o, w_hy_o, w_out):
    m = jax.nn.sigmoid(g_na) * (y_na @ w_na_o) + jax.nn.sigmoid(g_hy) * (y_hy @ w_hy_o)
    return m @ w_out


def swiglu(h, w1, w3, w2):
    return (jax.nn.silu(h @ w1) * (h @ w3)) @ w2


def setup_inputs(seed: int = 0) -> dict:
    key = jax.random.key(seed)
    ks = jax.random.split(key, 32)
    f32 = jnp.float32

    def nrm(k, shape, scale):
        return jax.random.normal(k, shape, f32) * scale

    return {
        'x': nrm(ks[0], (BATCH, SEQ, D_MODEL), 1.0),
        'c': nrm(ks[1], (BATCH, D_MODEL), 1.0),
        'ctx': nrm(ks[2], (BATCH, CTX_LEN, D_MODEL), 1.0),
        'c_ctx': nrm(ks[3], (D_MODEL,), 1.0),
        'w_ada': nrm(ks[4], (DEPTH, D_MODEL, N_MOD * D_MODEL), D_MODEL ** -0.5),
        'b_ada': nrm(ks[5], (DEPTH, N_MOD * D_MODEL), 0.02),
        'norm1_g': 1.0 + nrm(ks[6], (DEPTH, D_MODEL), 0.02),
        'norm2_g': 1.0 + nrm(ks[7], (DEPTH, D_MODEL), 0.02),
        'w_in': nrm(ks[8], (DEPTH, D_MODEL, IN_WIDTH), D_MODEL ** -0.5),
        'na_rpb': nrm(ks[9], (DEPTH, NA_HEADS, 2 * NA_WIN_H - 1, 2 * NA_WIN_W - 1), 0.1),
        'hy_conv_w': nrm(ks[10], (DEPTH, HY_SHORT, HY_PROJ), HY_SHORT ** -0.5),
        'hy_conv_b': nrm(ks[11], (DEPTH, HY_PROJ), 0.02),
        'hy_ffn_w1': nrm(ks[12], (DEPTH, HY_EMB, HY_FFN), HY_EMB ** -0.5),
        'hy_ffn_b1': nrm(ks[13], (DEPTH, HY_FFN), 0.1),
        'hy_ffn_w2': nrm(ks[14], (DEPTH, HY_FFN, HY_FFN), HY_FFN ** -0.5),
        'hy_ffn_b2': nrm(ks[15], (DEPTH, HY_FFN), 0.1),
        'hy_sin_freq': 1.0 + nrm(ks[16], (DEPTH, HY_FFN), 0.05),
        'hy_ffn_w3': nrm(ks[17], (DEPTH, HY_FFN, 2 * HY_ORDER * HY_WIDTH), HY_FILTER_SCALE),
        'hy_bias': nrm(ks[18], (DEPTH, HY_ORDER, HY_WIDTH), 0.2),
        'w_na_o': nrm(ks[19], (DEPTH, NA_WIDTH, D_MODEL), NA_WIDTH ** -0.5),
        'w_hy_o': nrm(ks[20], (DEPTH, HY_WIDTH, D_MODEL), HY_WIDTH ** -0.5),
        'w_out': nrm(ks[21], (DEPTH, D_MODEL, D_MODEL), D_MODEL ** -0.5),
        'ffn_w1': nrm(ks[22], (DEPTH, D_MODEL, FFN_HIDDEN), D_MODEL ** -0.5),
        'ffn_w3': nrm(ks[23], (DEPTH, D_MODEL, FFN_HIDDEN), D_MODEL ** -0.5),
        'ffn_w2': nrm(ks[24], (DEPTH, FFN_HIDDEN, D_MODEL), FFN_HIDDEN ** -0.5),
        'final_g': 1.0 + nrm(ks[25], (D_MODEL,), 0.02),
    }


def reference(x, c, ctx, c_ctx, w_ada, b_ada, norm1_g, norm2_g, w_in, na_rpb, hy_conv_w, hy_conv_b,
              hy_ffn_w1, hy_ffn_b1, hy_ffn_w2, hy_ffn_b2, hy_sin_freq, hy_ffn_w3, hy_bias,
              w_na_o, w_hy_o, w_out, ffn_w1, ffn_w3, ffn_w2, final_g):
    xc = ctx
    for i in range(DEPTH):
        last = i == DEPTH - 1
        mod = jax.nn.silu(c) @ w_ada[i] + b_ada[i]
        mod_c = jax.nn.silu(c_ctx) @ w_ada[i] + b_ada[i]
        sh1, sc1, g1, sh2, sc2, g2 = jnp.split(mod[:, None, :], N_MOD, axis=-1)
        csh1, csc1, cg1, csh2, csc2, cg2 = jnp.split(mod_c, N_MOD, axis=-1)
        hy_params = (hy_conv_w[i], hy_conv_b[i], hy_ffn_w1[i], hy_ffn_b1[i], hy_ffn_w2[i], hy_ffn_b2[i],
                     hy_sin_freq[i], hy_ffn_w3[i], hy_bias[i])

        hc = modulate(rmsnorm(xc, norm1_g[i]), csh1, csc1)
        if last:
            k_c, v_c = jnp.split(hc @ w_in[i][:, NA_WIDTH:3 * NA_WIDTH], 2, axis=-1)
        else:
            q_c, k_c, v_c, hv_c, hx1_c, hx2_c, gna_c, ghy_c = jnp.split(hc @ w_in[i], IN_SPLITS, axis=-1)
            y_na_c = context_attention(to_heads(q_c), to_heads(k_c), to_heads(v_c))
            y_hy_c = hyena_mixer(hv_c, hx1_c, hx2_c, *hy_params)
            xc_next = xc + cg1 * gated_merge(gna_c, ghy_c, y_na_c, y_hy_c, w_na_o[i], w_hy_o[i], w_out[i])
            hc2 = modulate(rmsnorm(xc_next, norm2_g[i]), csh2, csc2)
            xc_next = xc_next + cg2 * swiglu(hc2, ffn_w1[i], ffn_w3[i], ffn_w2[i])

        h = modulate(rmsnorm(x, norm1_g[i]), sh1, sc1)
        q, k, v, hv, hx1, hx2, gna, ghy = jnp.split(h @ w_in[i], IN_SPLITS, axis=-1)
        y_na = neighbourhood_attention(to_heads(q), to_heads(k), to_heads(v),
                                       to_heads(k_c), to_heads(v_c), na_rpb[i])
        y_hy = hyena_mixer(hv, hx1, hx2, *hy_params)
        x = x + g1 * gated_merge(gna, ghy, y_na, y_hy, w_na_o[i], w_hy_o[i], w_out[i])
        h2 = modulate(rmsnorm(x, norm2_g[i]), sh2, sc2)
        x = x + g2 * swiglu(h2, ffn_w1[i], ffn_w3[i], ffn_w2[i])
        if not last:
            xc = xc_next
    return rmsnorm(x, final_g)
```

```python
import cmath
import functools
import math

import jax
import jax.numpy as jnp
from jax import lax
from jax.experimental import pallas as pl
from jax.experimental.pallas import tpu as pltpu

F32 = jnp.float32
BF16 = jnp.bfloat16

D = 1024
L = 4096
GW = 64
CTX = 256
DH = 64
NH = 8
NAW = NH * DH
HYW = 512
FFN = 2816
NMOD = 6
EPS = 1e-6
NEG = -1e30
THETA = 10000.0
HY_EMB = 33
HY_BANDS = 16
HY_FFN = 64

NFFT = 2 * L
NC = 16
NB = 256
TM = 512
CB = 128
QR = 8
BAND = 16
VMEM_LIMIT = 56 * 1024 * 1024


def _cp(sem):
    return pltpu.CompilerParams(dimension_semantics=sem, vmem_limit_bytes=VMEM_LIMIT)


def _const_spec(shape):
    nd = len(shape)
    return pl.BlockSpec(shape, lambda *_: (0,) * nd, pipeline_mode=pl.Buffered(1))


def _dot(a, b):
    return jnp.dot(a, b, preferred_element_type=F32)


def _dot_nt(a, b):
    return lax.dot_general(a, b, (((1,), (1,)), ((), ())), preferred_element_type=F32)


def _dot_tn(a, b):
    return lax.dot_general(a, b, (((0,), (0,)), ((), ())), preferred_element_type=F32)


def _split(a):
    hi = a.astype(BF16)
    lo = (a - hi.astype(F32)).astype(BF16)
    return hi, lo


def _dot3(a, b, dot=_dot):
    ah, al = _split(a)
    bh, bl = _split(b)
    return dot(ah, bh) + dot(al, bh) + dot(ah, bl)


def _sigmoid(x):
    return 1.0 / (1.0 + jnp.exp(-x))


def _rms_mod(x, g, shift, scale):
    ms = jnp.mean(x * x, axis=-1, keepdims=True)
    y = x * lax.rsqrt(ms + EPS) * g
    return y * (1.0 + scale) + shift


def _mod_kernel(c_ref, w_ref, b_ref, o_ref):
    c = c_ref[...]
    s = c * _sigmoid(c)
    o_ref[...] = _dot3(s, w_ref[...]) + b_ref[...]


def _mod_call(cvec, w_ada, b_ada):
    nt = 4
    tn = NMOD * D // nt
    return pl.pallas_call(
        _mod_kernel,
        grid=(nt,),
        in_specs=[pl.BlockSpec((16, D), lambda i: (0, 0)),
                  pl.BlockSpec((D, tn), lambda i: (0, i)),
                  pl.BlockSpec((1, tn), lambda i: (0, i))],
        out_specs=pl.BlockSpec((16, tn), lambda i: (0, i)),
        out_shape=jax.ShapeDtypeStruct((16, NMOD * D), F32),
        compiler_params=_cp(("arbitrary",)),
        name="mod",
    )(cvec, w_ada, b_ada)


def _ctx_kernel(x_ref, mod_ref, g_ref, w_ref, k_ref, v_ref):
    m = mod_ref[0]
    h = _rms_mod(x_ref[0], g_ref[...], m[:, 0:D], m[:, D:2 * D])
    kv = _dot(h.astype(BF16), w_ref[...])
    k_ref[0] = kv[:, :NAW].astype(BF16)
    v_ref[0] = kv[:, NAW:].astype(BF16)


def _ctx_call(ctx, mod3, g1, w_kv, nb):
    return pl.pallas_call(
        _ctx_kernel,
        grid=(nb,),
        in_specs=[pl.BlockSpec((1, CTX, D), lambda b: (b, 0, 0)),
                  pl.BlockSpec((1, 1, NMOD * D), lambda b: (nb, 0, 0)),
                  pl.BlockSpec((1, D), lambda b: (0, 0)),
                  _const_spec((D, 2 * NAW))],
        out_specs=[pl.BlockSpec((1, CTX, NAW), lambda b: (b, 0, 0)),
                   pl.BlockSpec((1, CTX, NAW), lambda b: (b, 0, 0))],
        out_shape=[jax.ShapeDtypeStruct((nb, CTX, NAW), BF16)] * 2,
        compiler_params=_cp(("arbitrary",)),
        name="ctx",
    )(ctx, mod3, g1, w_kv)


def _rope(t, cos, sin_signed, first):
    up = pltpu.roll(t, NAW - 16, axis=1)
    down = pltpu.roll(t, 16, axis=1)
    return t * cos + jnp.where(first, up, down) * sin_signed


def _inproj_kernel(x_ref, mod_ref, g_ref, wa_ref, wh_ref, cos_ref, sin_ref,
                   q_ref, qr_ref, kr_ref, v_ref, gna_ref, ghy_ref, ut_ref):
    m = mod_ref[0]
    hb = _rms_mod(x_ref[0], g_ref[...], m[:, 0:D], m[:, D:2 * D]).astype(BF16)
    cos = cos_ref[...]
    sin = sin_ref[...]
    lane = lax.broadcasted_iota(jnp.int32, (TM, NAW), 1)
    first = (lane & 16) == 0
    q = _dot(hb, wa_ref[:, 0:NAW]) * 0.125
    q_ref[0] = q.astype(BF16)
    qr_ref[0] = _rope(q, cos, sin, first).astype(BF16)
    k = _dot(hb, wa_ref[:, NAW:2 * NAW])
    kr_ref[0] = _rope(k, cos, sin, first).astype(BF16)
    v_ref[0] = _dot(hb, wa_ref[:, 2 * NAW:3 * NAW]).astype(BF16)
    gna_ref[0] = _dot(hb, wa_ref[:, 3 * NAW:3 * NAW + D]).astype(BF16)
    ghy_ref[0] = _dot(hb, wa_ref[:, 3 * NAW + D:3 * NAW + 2 * D]).astype(BF16)
    ut_ref[0] = _dot_nt(wh_ref[...], hb)


def _inproj_call(x, mod3, g1, w_a, w_hyt, cos, sin, nb):
    nt = L // TM
    tok = lambda w: pl.BlockSpec((1, TM, w), lambda i, b: (b, i, 0))
    return pl.pallas_call(
        _inproj_kernel,
        grid=(nt, nb),
        in_specs=[tok(D),
                  pl.BlockSpec((1, 1, NMOD * D), lambda i, b: (b, 0, 0)),
                  pl.BlockSpec((1, D), lambda i, b: (0, 0)),
                  _const_spec((D, 3 * NAW + 2 * D)),
                  _const_spec((3 * HYW, D)),
                  pl.BlockSpec((TM, NAW), lambda i, b: (i, 0)),
                  pl.BlockSpec((TM, NAW), lambda i, b: (i, 0))],
        out_specs=[tok(NAW), tok(NAW), tok(NAW), tok(NAW), tok(D), tok(D),
                   pl.BlockSpec((1, 3 * HYW, TM), lambda i, b: (b, 0, i))],
        out_shape=[jax.ShapeDtypeStruct((nb, L, NAW), BF16)] * 4
        + [jax.ShapeDtypeStruct((nb, L, D), BF16)] * 2
        + [jax.ShapeDtypeStruct((nb, 3 * HYW, L), F32)],
        compiler_params=_cp(("arbitrary", "arbitrary")),
        name="inproj",
    )(x, mod3, g1, w_a, w_hyt, cos, sin)


def _na_geometry(r0, start):
    rows = L // GW
    geo = {}
    for ri in range(QR):
        r = r0 + ri
        ws = min(max(r - 4, 0), rows - 8)
        for bi in range(BAND):
            kr = start + bi
            geo[ri, bi] = (ws <= kr < ws + 8, kr - r + 7)
    return geo


def _na_block(geo, start_tok, q_ref, qr_ref, k_ref, v_ref, kc_ref, vc_ref, t2_ref,
              o_ref, s_ref, p_ref, pc_ref):
    kb = k_ref[0, pl.ds(start_tok, BAND * GW), :]
    vb = v_ref[0, pl.ds(start_tok, BAND * GW), :]
    kc = kc_ref[0]
    vc = vc_ref[0]
    q = q_ref[0]
    qr = qr_ref[0]
    nq = QR * GW
    lane_q = lax.broadcasted_iota(jnp.int32, (nq, 2 * DH), 1)
    lane_t = lax.broadcasted_iota(jnp.int32, (GW, 2 * DH), 1)
    left = lane_t < DH
    outs = []
    for hh in range(2):
        mine = (lane_q < DH) if hh == 0 else (lane_q >= DH)
        zero = jnp.zeros_like(q)
        s_ref[...] = _dot_nt(jnp.where(mine, qr, zero), kb)
        sc = _dot_nt(jnp.where(mine, q, zero), kc)
        dens = []
        for ri in range(QR):
            rs = slice(GW * ri, GW * (ri + 1))
            sc_r = sc[rs]
            mvec = jnp.maximum(sc_r[:, :2 * DH], sc_r[:, 2 * DH:])
            tiles = {}
            for v in range(BAND // 2):
                vl, rl = geo[ri, 2 * v]
                vr, _ = geo[ri, 2 * v + 1]
                if not (vl or vr):
                    continue
                t = s_ref[rs, 2 * DH * v:2 * DH * (v + 1)] + t2_ref[hh, rl + 1]
                if not (vl and vr):
                    t = jnp.where(left if vl else jnp.logical_not(left), t, NEG)
                tiles[v] = t
                mvec = jnp.maximum(mvec, t)
            mx = jnp.max(mvec, axis=-1, keepdims=True)
            pc = jnp.exp(sc_r - mx)
            pc_ref[rs, :] = pc.astype(BF16)
            acc = pc[:, :2 * DH] + pc[:, 2 * DH:]
            for v in range(BAND // 2):
                cs = slice(2 * DH * v, 2 * DH * (v + 1))
                if v in tiles:
                    p = jnp.exp(tiles[v] - mx)
                    acc = acc + p
                    p_ref[rs, cs] = p.astype(BF16)
                else:
                    p_ref[rs, cs] = jnp.zeros((GW, 2 * DH), BF16)
            dens.append(jnp.sum(acc, axis=-1, keepdims=True))
        o = _dot(p_ref[...], vb) + _dot(pc_ref[...], vc)
        outs.append(o / jnp.concatenate(dens, axis=0))
    o_ref[0] = jnp.where(lane_q < DH, outs[0], outs[1]).astype(BF16)


def _na_kernel(q_ref, qr_ref, k_ref, v_ref, kc_ref, vc_ref, t2_ref, o_ref,
               s_ref, p_ref, pc_ref):
    j = pl.program_id(2)
    nj = L // (QR * GW)
    args = (q_ref, qr_ref, k_ref, v_ref, kc_ref, vc_ref, t2_ref, o_ref, s_ref, p_ref, pc_ref)

    @pl.when(j == 0)
    def _():
        _na_block(_na_geometry(0, 0), 0, *args)

    @pl.when(jnp.logical_and(j > 0, j < nj - 1))
    def _():
        start_tok = pl.multiple_of((QR * j - 4) * GW, 4 * GW)
        _na_block(_na_geometry(QR, QR - 4), start_tok, *args)

    @pl.when(j == nj - 1)
    def _():
        rows = L // GW
        _na_block(_na_geometry(rows - QR, rows - BAND), (rows - BAND) * GW, *args)


def _na_call(q, qr, kr, v, kc, vc, t2, nb):
    nq = QR * GW
    nj = L // nq
    blk = lambda: pl.BlockSpec((1, nq, 2 * DH), lambda b, hp, j: (b, j, hp))
    full = lambda n: pl.BlockSpec((1, n, 2 * DH), lambda b, hp, j: (b, 0, hp))
    return pl.pallas_call(
        _na_kernel,
        grid=(nb, NH // 2, nj),
        in_specs=[blk(), blk(), full(L), full(L), full(CTX), full(CTX),
                  pl.BlockSpec((2, 16, GW, 2 * DH), lambda b, hp, j: (hp, 0, 0, 0))],
        out_specs=blk(),
        out_shape=jax.ShapeDtypeStruct((nb, L, NAW), BF16),
        scratch_shapes=[pltpu.VMEM((nq, BAND * GW), F32),
                        pltpu.VMEM((nq, BAND * GW), BF16),
                        pltpu.VMEM((nq, CTX), BF16)],
        compiler_params=_cp(("arbitrary", "arbitrary", "arbitrary")),
        name="na",
    )(q, qr, kr, v, kc, vc, t2)


_R2 = math.sqrt(0.5)


def _bfly(e, o, k, n, sign):
    er, ei = e
    orr, oi = o
    if k == 0:
        tr, ti = orr, oi
    elif 4 * k == n:
        if sign < 0:
            return (er + oi, ei - orr), (er - oi, ei + orr)
        return (er - oi, ei + orr), (er + oi, ei - orr)
    elif 8 * k == n:
        if sign < 0:
            tr, ti = (orr + oi) * _R2, (oi - orr) * _R2
        else:
            tr, ti = (orr - oi) * _R2, (oi + orr) * _R2
    elif 8 * k == 3 * n:
        if sign < 0:
            tr, ti = (oi - orr) * _R2, (orr + oi) * (-_R2)
        else:
            tr, ti = (orr + oi) * (-_R2), (orr - oi) * _R2
    else:
        w = cmath.exp(sign * 2j * math.pi * k / n)
        tr = orr * w.real - oi * w.imag
        ti = orr * w.imag + oi * w.real
    return (er + tr, ei + ti), (er - tr, ei - ti)


def _fft(xs, sign):
    n = len(xs)
    if n == 1:
        return xs
    ev = _fft(xs[0::2], sign)
    od = _fft(xs[1::2], sign)
    out = [None] * n
    for k in range(n // 2):
        out[k], out[k + n // 2] = _bfly(ev[k], od[k], k, n, sign)
    return out


def _coarse_fwd(x_ref, a_ref, twr_ref, twi_ref, nrows, nblocks):
    def body(i, carry):
        r = pl.multiple_of(i * 8, 8)
        rows = pl.ds(r, 8)
        for c in range(NB // 128):
            xs = []
            for n1 in range(NC):
                lo = n1 * NB + c * 128
                x = x_ref[rows, lo:lo + 128]
                w = cmath.exp(-2j * math.pi * n1 / (4 * NC))
                if nblocks == 2 * NC:
                    lo2 = (n1 + NC) * NB + c * 128
                    x2 = x_ref[rows, lo2:lo2 + 128]
                    w2 = cmath.exp(-2j * math.pi * (n1 + NC) / (4 * NC))
                    xs.append((x * w.real + x2 * w2.real, x * w.imag + x2 * w2.imag))
                elif n1 == 0:
                    xs.append((x, jnp.zeros_like(x)))
                else:
                    xs.append((x * w.real, x * w.imag))
            ys = _fft(xs, -1)
            for m in range(NC):
                cs = slice(c * 128, (c + 1) * 128)
                tr = twr_ref[m, :, cs]
                ti = twi_ref[m, :, cs]
                ar, ai = ys[m]
                a_ref[m, rows, c * 128:(c + 1) * 128] = ar * tr - ai * ti
                a_ref[m, rows, NB + c * 128:NB + (c + 1) * 128] = ar * ti + ai * tr
        return carry

    lax.fori_loop(0, nrows // 8, body, 0)


def _coarse_inv(b_ref, twr_ref, twi_ref, z_ref, gate_ref, bias_ref, o_ref, nrows):
    def body(i, carry):
        r = pl.multiple_of(i * 8, 8)
        rows = pl.ds(r, 8)
        bias = bias_ref[rows, :]
        for c in range(NB // 128):
            cs = slice(c * 128, (c + 1) * 128)
            bs = []
            for m in range(NC):
                br = b_ref[m, rows, c * 128:(c + 1) * 128]
                bi = b_ref[m, rows, NB + c * 128:NB + (c + 1) * 128]
                tr = twr_ref[m, :, cs]
                ti = twi_ref[m, :, cs]
                bs.append((br * tr + bi * ti, bi * tr - br * ti))
            ys = _fft(bs, +1)
            for n1 in range(NC):
                w = cmath.exp(2j * math.pi * n1 / (4 * NC))
                yr, yi = ys[n1]
                y = yr if n1 == 0 else yr * w.real - yi * w.imag
                lo = n1 * NB + c * 128
                z = z_ref[rows, lo:lo + 128]
                o_ref[rows, lo:lo + 128] = gate_ref[rows, lo:lo + 128] * (y + bias * z)
        return carry

    lax.fori_loop(0, nrows // 8, body, 0)


def _hymlp_kernel(zt_ref, w1_ref, b1_ref, w2_ref, b2_ref, fr_ref, o_ref):
    fr = fr_ref[...]
    h1 = jnp.sin(fr * (_dot3(w1_ref[...], zt_ref[...]) + b1_ref[...]))
    o_ref[...] = jnp.sin(fr * (_dot3(w2_ref[...], h1) + b2_ref[...]))


def _hymlp_call(zt, w1t, b1, w2t, b2, fr):
    full = lambda s: pl.BlockSpec(s, lambda i: (0,) * len(s))
    return pl.pallas_call(
        _hymlp_kernel,
        grid=(1,),
        in_specs=[full(zt.shape), full(w1t.shape), full(b1.shape),
                  full(w2t.shape), full(b2.shape), full(fr.shape)],
        out_specs=full((HY_FFN, NFFT)),
        out_shape=jax.ShapeDtypeStruct((HY_FFN, NFFT), F32),
        compiler_params=_cp(("arbitrary",)),
        name="hymlp",
    )(zt, w1t, b1, w2t, b2, fr)


def _filt_kernel(h_ref, wf_ref, wb_ref, delta_ref, tpos_ref, sgn_ref, twr_ref, twi_ref,
                 fc_ref, kf_ref, kt_ref, a_ref):
    ht = h_ref[...]
    dec = jnp.exp(-delta_ref[...] * tpos_ref[...]) * sgn_ref[...]
    kt_ref[:, :L] = _dot3(wf_ref[...], ht[:, :L]) * dec[:, :L]
    kt_ref[:, L:] = _dot3(wb_ref[...], ht[:, L:]) * dec[:, L:]
    _coarse_fwd(kt_ref, a_ref, twr_ref, twi_ref, CB, 2 * NC)
    a2 = a_ref[...].reshape(NC * CB, 2 * NB)
    kf_ref[0] = (_dot3(a2, fc_ref[...]) * (2.0 / NFFT)).reshape(NC, CB, 2 * NB)


def _filt_call(hidt, w3f, w3b, delta, tpos, sgn, twr, twi, fc):
    nblk = 2 * HYW // CB
    per = HYW // CB
    return pl.pallas_call(
        _filt_kernel,
        grid=(nblk,),
        in_specs=[_const_spec((HY_FFN, NFFT)),
                  pl.BlockSpec((CB, HY_FFN), lambda i: (i, 0)),
                  pl.BlockSpec((CB, HY_FFN), lambda i: (i, 0)),
                  pl.BlockSpec((CB, 1), lambda i: (i, 0)),
                  _const_spec((1, NFFT)),
                  _const_spec((1, NFFT)),
                  _const_spec((NC, 8, NB)),
                  _const_spec((NC, 8, NB)),
                  _const_spec((2 * NB, 2 * NB))],
        out_specs=pl.BlockSpec((1, NC, CB, 2 * NB), lambda i: (i // per, 0, i % per, 0)),
        out_shape=jax.ShapeDtypeStruct((2, NC, HYW, 2 * NB), F32),
        scratch_shapes=[pltpu.VMEM((CB, NFFT), F32),
                        pltpu.VMEM((NC, CB, 2 * NB), F32)],
        compiler_params=_cp(("arbitrary",)),
        name="filt",
    )(hidt, w3f, w3b, delta, tpos, sgn, twr, twi, fc)


def _short_conv(u_ref, w_ref, b_ref, o_ref, nrows):
    def body(i, carry):
        r = pl.multiple_of(i * 8, 8)
        rows = pl.ds(r, 8)
        u = u_ref[0, rows, :]
        lane = lax.broadcasted_iota(jnp.int32, (8, L), 1)
        prev = jnp.where(lane == 0, 0.0, pltpu.roll(u, 1, axis=1))
        nxt = jnp.where(lane == L - 1, 0.0, pltpu.roll(u, L - 1, axis=1))
        w = w_ref[rows, :]
        o_ref[rows, :] = (prev * w[:, 0:1] + u * w[:, 1:2] + nxt * w[:, 2:3]
                          + b_ref[rows, 0:1])
        return carry

    lax.fori_loop(0, nrows // 8, body, 0)


def _spectral_mul(x_ref, kf_ref, order, y_ref):
    for m in range(NC):
        xr = x_ref[m, :, :NB]
        xi = x_ref[m, :, NB:]
        kr = kf_ref[order, m, :, :NB]
        ki = kf_ref[order, m, :, NB:]
        y_ref[m, :, :NB] = xr * kr - xi * ki
        y_ref[m, :, NB:] = xr * ki + xi * kr


def _hyena_kernel(v_ref, x1_ref, x2_ref, cwv_ref, cw1_ref, cw2_ref, cbv_ref, cb1_ref, cb2_ref,
                  kf_ref, bias_ref, twr_ref, twi_ref, fc_ref, fi_ref, o_ref,
                  z_ref, g1_ref, g2_ref, a_ref, x_ref):
    _short_conv(v_ref, cwv_ref, cbv_ref, z_ref, CB)
    _short_conv(x1_ref, cw1_ref, cb1_ref, g1_ref, CB)
    _short_conv(x2_ref, cw2_ref, cb2_ref, g2_ref, CB)
    for order, (gate_ref, dst_ref) in enumerate(((g1_ref, z_ref), (g2_ref, o_ref.at[0]))):
        _coarse_fwd(z_ref, a_ref, twr_ref, twi_ref, CB, NC)
        a2 = a_ref[...].reshape(NC * CB, 2 * NB).astype(BF16)
        x_ref[...] = _dot(a2, fc_ref[...]).reshape(NC, CB, 2 * NB)
        _spectral_mul(x_ref, kf_ref, order, a_ref)
        y2 = a_ref[...].reshape(NC * CB, 2 * NB).astype(BF16)
        x_ref[...] = _dot(y2, fi_ref[...]).reshape(NC, CB, 2 * NB)
        _coarse_inv(x_ref, twr_ref, twi_ref, z_ref, gate_ref, bias_ref.at[order], dst_ref, CB)


def _hyena_call(ut, cw, cb, kf, bias, twr, twi, fc, fi, nb):
    per = HYW // CB
    chan = lambda part: pl.BlockSpec((1, CB, L), lambda c, b: (b, part * per + c, 0))
    colw = lambda part: pl.BlockSpec((CB, 128), lambda c, b: (part * per + c, 0))
    return pl.pallas_call(
        _hyena_kernel,
        grid=(per, nb),
        in_specs=[chan(0), chan(1), chan(2),
                  colw(0), colw(1), colw(2), colw(0), colw(1), colw(2),
                  pl.BlockSpec((2, NC, CB, 2 * NB), lambda c, b: (0, 0, c, 0),
                               pipeline_mode=pl.Buffered(1)),
                  pl.BlockSpec((2, CB, 128), lambda c, b: (0, c, 0)),
                  _const_spec((NC, 8, NB)),
                  _const_spec((NC, 8, NB)),
                  _const_spec((2 * NB, 2 * NB)),
                  _const_spec((2 * NB, 2 * NB))],
        out_specs=pl.BlockSpec((1, CB, L), lambda c, b: (b, c, 0)),
        out_shape=jax.ShapeDtypeStruct((nb, HYW, L), F32),
        scratch_shapes=[pltpu.VMEM((CB, L), F32),
                        pltpu.VMEM((CB, L), F32),
                        pltpu.VMEM((CB, L), F32),
                        pltpu.VMEM((NC, CB, 2 * NB), F32),
                        pltpu.VMEM((NC, CB, 2 * NB), F32)],
        compiler_params=_cp(("arbitrary", "arbitrary")),
        name="hyena",
    )(ut, ut, ut, cw, cw, cw, cb, cb, cb, kf, bias, twr, twi, fc, fi)


def _merge_kernel(x_ref, yna_ref, yhy_ref, gna_ref, ghy_ref, mod_ref, g_ref,
                  wna_ref, why_ref, wout_ref, x1_ref, h2_ref):
    m = mod_ref[0]
    a = _dot(yna_ref[0], wna_ref[...])
    b = _dot_tn(yhy_ref[0].astype(BF16), why_ref[...])
    mix = (_sigmoid(gna_ref[0].astype(F32)) * a + _sigmoid(ghy_ref[0].astype(F32)) * b)
    x1 = x_ref[0] + m[:, 2 * D:3 * D] * _dot(mix.astype(BF16), wout_ref[...])
    x1_ref[0] = x1
    h2_ref[0] = _rms_mod(x1, g_ref[...], m[:, 3 * D:4 * D], m[:, 4 * D:5 * D]).astype(BF16)


def _merge_call(x, yna, yhyt, gna, ghy, mod3, g2, wna, why, wout, nb):
    nt = L // TM
    tok = lambda w: pl.BlockSpec((1, TM, w), lambda b, i: (b, i, 0))
    return pl.pallas_call(
        _merge_kernel,
        grid=(nb, nt),
        in_specs=[tok(D), tok(NAW),
                  pl.BlockSpec((1, HYW, TM), lambda b, i: (b, 0, i)),
                  tok(D), tok(D),
                  pl.BlockSpec((1, 1, NMOD * D), lambda b, i: (b, 0, 0)),
                  pl.BlockSpec((1, D), lambda b, i: (0, 0)),
                  _const_spec((NAW, D)), _const_spec((HYW, D)), _const_spec((D, D))],
        out_specs=[tok(D), tok(D)],
        out_shape=[jax.ShapeDtypeStruct((nb, L, D), F32),
                   jax.ShapeDtypeStruct((nb, L, D), BF16)],
        compiler_params=_cp(("arbitrary", "arbitrary")),
        name="merge",
    )(x, yna, yhyt, gna, ghy, mod3, g2, wna, why, wout)


def _ffn_kernel(x1_ref, h2_ref, mod_ref, gf_ref, w1_ref, w3_ref, w2_ref, o_ref):
    m = mod_ref[0]
    h2 = h2_ref[0]
    a = _dot(h2, w1_ref[...])
    b = _dot(h2, w3_ref[...])
    act = (a * _sigmoid(a) * b).astype(BF16)
    x2 = x1_ref[0] + m[:, 5 * D:6 * D] * _dot(act, w2_ref[...])
    ms = jnp.mean(x2 * x2, axis=-1, keepdims=True)
    o_ref[0] = x2 * lax.rsqrt(ms + EPS) * gf_ref[...]


def _ffn_call(x1, h2, mod3, gf, w1, w3, w2, nb):
    nt = L // TM
    tok = lambda: pl.BlockSpec((1, TM, D), lambda b, i: (b, i, 0))
    return pl.pallas_call(
        _ffn_kernel,
        grid=(nb, nt),
        in_specs=[tok(), tok(),
                  pl.BlockSpec((1, 1, NMOD * D), lambda b, i: (b, 0, 0)),
                  pl.BlockSpec((1, D), lambda b, i: (0, 0)),
                  _const_spec((D, FFN)), _const_spec((D, FFN)), _const_spec((FFN, D))],
        out_specs=tok(),
        out_shape=jax.ShapeDtypeStruct((nb, L, D), F32),
        compiler_params=_cp(("arbitrary", "arbitrary")),
        name="ffn",
    )(x1, h2, mod3, gf, w1, w3, w2)


def _rope_tables():
    t = jnp.arange(L)
    rows = (t // GW).astype(F32)
    cols = (t % GW).astype(F32)
    nf = DH // 4
    inv = THETA ** (-jnp.arange(nf, dtype=F32) / nf)
    ar = rows[:, None] * inv
    ac = cols[:, None] * inv
    cos = jnp.concatenate([jnp.cos(ar), jnp.cos(ar), jnp.cos(ac), jnp.cos(ac)], axis=-1)
    sin = jnp.concatenate([-jnp.sin(ar), jnp.sin(ar), -jnp.sin(ac), jnp.sin(ac)], axis=-1)
    return jnp.tile(cos, (1, NH)), jnp.tile(sin, (1, NH))


def _bias_tables(rpb):
    cq = jnp.arange(GW)
    start_c = jnp.clip(cq - 8, 0, GW - 16)
    col_ok = (cq[None, :] >= start_c[:, None]) & (cq[None, :] < start_c[:, None] + 16)
    coff = jnp.clip(cq[None, :] - cq[:, None], -15, 15) + 15
    g = jnp.where(col_ok[None, None], rpb.astype(F32)[:, :, coff], NEG)
    pad = jnp.full((NH, 1, GW, GW), NEG, F32)
    gx = jnp.concatenate([pad, g, pad], axis=1)
    return jnp.concatenate([gx[:, :16], gx[:, 1:17]], axis=-1)


def _dft_tables():
    n2 = jnp.arange(NB)
    ang = (2.0 * math.pi / NB) * ((n2[:, None] * n2[None, :]) % NB).astype(F32)
    fr = jnp.cos(ang)
    fi = -jnp.sin(ang)
    fwd = jnp.concatenate([jnp.concatenate([fr, fi], 1), jnp.concatenate([-fi, fr], 1)], 0)
    inv = jnp.concatenate([jnp.concatenate([fr, -fi], 1), jnp.concatenate([fi, fr], 1)], 0)
    m = jnp.arange(NC)
    tang = (2.0 * math.pi / (2 * NFFT)) * ((n2[None, :] * (4 * m[:, None] + 1)) % (2 * NFFT)).astype(F32)
    twr = jnp.broadcast_to(jnp.cos(tang)[:, None, :], (NC, 8, NB))
    twi = jnp.broadcast_to(-jnp.sin(tang)[:, None, :], (NC, 8, NB))
    return fwd, inv, twr, twi


def _filter_inputs():
    t = jnp.linspace(0.0, 1.0, L, dtype=F32)[:, None]
    w = 2.0 * math.pi * jnp.arange(L, dtype=F32)[:, None] / L
    bands = jnp.linspace(1e-4, HY_BANDS - 1, HY_BANDS, dtype=F32)
    z = jnp.concatenate([t, jnp.cos(bands * w), jnp.sin(-bands * w)], axis=-1)
    zb = jnp.concatenate([z[0:1], z[:0:-1]], axis=0)
    zt = jnp.concatenate([z, zb], axis=0).T
    zt = jnp.pad(zt, ((0, HY_FFN - HY_EMB), (0, 0)))
    tl = t[:, 0]
    tpos = jnp.concatenate([tl, tl[0:1], tl[:0:-1]])[None, :]
    sgn = jnp.concatenate([jnp.ones((L,), F32), jnp.zeros((1,), F32), -jnp.ones((L - 1,), F32)])[None, :]
    min_decay = math.log(1e-2) / 1.5
    max_decay = math.log(1e-2) / 0.3
    deltas = jnp.abs(jnp.linspace(min_decay, max_decay, HYW, dtype=F32))
    return zt, tpos, sgn, deltas


def kernel(x, c, ctx, c_ctx, w_ada, b_ada, norm1_g, norm2_g, w_in, na_rpb, hy_conv_w, hy_conv_b,
           hy_ffn_w1, hy_ffn_b1, hy_ffn_w2, hy_ffn_b2, hy_sin_freq, hy_ffn_w3, hy_bias,
           w_na_o, w_hy_o, w_out, ffn_w1, ffn_w3, ffn_w2, final_g):
    nb = x.shape[0]
    assert x.shape[1:] == (L, D) and w_ada.shape[0] == 1 and nb < 16

    cvec = jnp.zeros((16, D), F32).at[:nb].set(c).at[nb].set(c_ctx)
    mod = _mod_call(cvec, w_ada[0], b_ada[0][None, :])
    mod3 = mod[:, None, :]

    wi = w_in[0]
    g1 = norm1_g[0][None, :]
    kc, vc = _ctx_call(ctx, mod3, g1, wi[:, NAW:3 * NAW].astype(BF16), nb)

    w_a = jnp.concatenate([wi[:, :3 * NAW], wi[:, 3 * NAW + 3 * HYW:]], axis=1).astype(BF16)
    w_hyt = wi[:, 3 * NAW:3 * NAW + 3 * HYW].T.astype(BF16)
    cos, sin = _rope_tables()
    q, qr, kr, v, gna, ghy, ut = _inproj_call(x, mod3, g1, w_a, w_hyt, cos, sin, nb)

    yna = _na_call(q, qr, kr, v, kc, vc, _bias_tables(na_rpb[0]), nb)

    fwd, inv, twr, twi = _dft_tables()
    zt, tpos, sgn, deltas = _filter_inputs()
    col = lambda a: a.astype(F32)[:, None]
    w1t = jnp.pad(hy_ffn_w1[0].T, ((0, 0), (0, HY_FFN - HY_EMB)))
    hidt = _hymlp_call(zt, w1t, col(hy_ffn_b1[0]), hy_ffn_w2[0].T, col(hy_ffn_b2[0]),
                       col(hy_sin_freq[0]))
    w3t = hy_ffn_w3[0].T
    kf = _filt_call(hidt, w3t[:2 * HYW], w3t[2 * HYW:], jnp.tile(deltas, 2)[:, None],
                    tpos, sgn, twr, twi, fwd)

    cw = jnp.pad(hy_conv_w[0].T, ((0, 0), (0, 125)))
    cb = jnp.broadcast_to(hy_conv_b[0][:, None], (3 * HYW, 128))
    bias = jnp.broadcast_to(hy_bias[0][:, :, None], (2, HYW, 128))
    yhyt = _hyena_call(ut, cw, cb, kf, bias, twr, twi, fwd.astype(BF16), inv.astype(BF16), nb)

    x1, h2 = _merge_call(x, yna, yhyt, gna, ghy, mod3, norm2_g[0][None, :],
                         w_na_o[0].astype(BF16), w_hy_o[0].astype(BF16), w_out[0].astype(BF16), nb)
    return _ffn_call(x1, h2, mod3, final_g[None, :], ffn_w1[0].astype(BF16),
                     ffn_w3[0].astype(BF16), ffn_w2[0].astype(BF16), nb)
```

```python
import cmath
import functools
import math

import jax
import jax.numpy as jnp
from jax import lax
from jax.experimental import pallas as pl
from jax.experimental.pallas import tpu as pltpu

F32 = jnp.float32
BF16 = jnp.bfloat16

D = 1024
L = 4096
GW = 64
CTX = 256
DH = 64
NH = 8
NAW = NH * DH
HYW = 512
FFN = 2816
NMOD = 6
EPS = 1e-6
NEG = -1e30
THETA = 10000.0
HY_EMB = 33
HY_BANDS = 16
HY_FFN = 64

NFFT = 2 * L
NC = 16
NB = 256
TM = 512
CB = 128
QR = 8
BAND = 16
QC = 16
KC = 32
NM = GW // QC
QSHIFT = 8
NT4 = 18
VMEM_LIMIT = 56 * 1024 * 1024


def _cp(sem):
    return pltpu.CompilerParams(dimension_semantics=sem, vmem_limit_bytes=VMEM_LIMIT)


def _const_spec(shape):
    nd = len(shape)
    return pl.BlockSpec(shape, lambda *_: (0,) * nd, pipeline_mode=pl.Buffered(1))


def _dot(a, b):
    return jnp.dot(a, b, preferred_element_type=F32)


def _dot_nt(a, b):
    return lax.dot_general(a, b, (((1,), (1,)), ((), ())), preferred_element_type=F32)


def _dot_tn(a, b):
    return lax.dot_general(a, b, (((0,), (0,)), ((), ())), preferred_element_type=F32)


def _split(a):
    hi = a.astype(BF16)
    lo = (a - hi.astype(F32)).astype(BF16)
    return hi, lo


def _dot3(a, b, dot=_dot):
    ah, al = _split(a)
    bh, bl = _split(b)
    return dot(ah, bh) + dot(al, bh) + dot(ah, bl)


def _sigmoid(x):
    return 1.0 / (1.0 + jnp.exp(-x))


def _rms_mod(x, g, shift, scale):
    ms = jnp.mean(x * x, axis=-1, keepdims=True)
    y = x * lax.rsqrt(ms + EPS) * g
    return y * (1.0 + scale) + shift


def _mod_kernel(c_ref, w_ref, b_ref, o_ref):
    c = c_ref[...]
    s = c * _sigmoid(c)
    o_ref[...] = _dot3(s, w_ref[...]) + b_ref[...]


def _mod_call(cvec, w_ada, b_ada):
    nt = 4
    tn = NMOD * D // nt
    return pl.pallas_call(
        _mod_kernel,
        grid=(nt,),
        in_specs=[pl.BlockSpec((16, D), lambda i: (0, 0)),
                  pl.BlockSpec((D, tn), lambda i: (0, i)),
                  pl.BlockSpec((1, tn), lambda i: (0, i))],
        out_specs=pl.BlockSpec((16, tn), lambda i: (0, i)),
        out_shape=jax.ShapeDtypeStruct((16, NMOD * D), F32),
        compiler_params=_cp(("arbitrary",)),
        name="mod",
    )(cvec, w_ada, b_ada)


def _ctx_kernel(x_ref, mod_ref, g_ref, w_ref, k_ref, v_ref):
    m = mod_ref[0]
    h = _rms_mod(x_ref[0], g_ref[...], m[:, 0:D], m[:, D:2 * D])
    kv = _dot(h.astype(BF16), w_ref[...])
    k_ref[0] = kv[:, :NAW].astype(BF16)
    v_ref[0] = kv[:, NAW:].astype(BF16)


def _ctx_call(ctx, mod3, g1, w_kv, nb):
    return pl.pallas_call(
        _ctx_kernel,
        grid=(nb,),
        in_specs=[pl.BlockSpec((1, CTX, D), lambda b: (b, 0, 0)),
                  pl.BlockSpec((1, 1, NMOD * D), lambda b: (nb, 0, 0)),
                  pl.BlockSpec((1, D), lambda b: (0, 0)),
                  _const_spec((D, 2 * NAW))],
        out_specs=[pl.BlockSpec((1, CTX, NAW), lambda b: (b, 0, 0)),
                   pl.BlockSpec((1, CTX, NAW), lambda b: (b, 0, 0))],
        out_shape=[jax.ShapeDtypeStruct((nb, CTX, NAW), BF16)] * 2,
        compiler_params=_cp(("arbitrary",)),
        name="ctx",
    )(ctx, mod3, g1, w_kv)


def _rope(t, cos, sin_signed, first):
    up = pltpu.roll(t, NAW - 16, axis=1)
    down = pltpu.roll(t, 16, axis=1)
    return t * cos + jnp.where(first, up, down) * sin_signed


def _rot_cols(t):
    t3 = t.reshape(TM // GW, GW, t.shape[-1])
    return jnp.concatenate([t3[:, QSHIFT:], t3[:, :QSHIFT]], axis=1).reshape(t.shape)


def _inproj_kernel(x_ref, mod_ref, g_ref, w_ref, cos_ref, sin_ref,
                   q_ref, qr_ref, kr_ref, v_ref, gna_ref, ghy_ref, u_ref):
    m = mod_ref[0]
    hb = _rms_mod(x_ref[0], g_ref[...], m[:, 0:D], m[:, D:2 * D]).astype(BF16)
    cos = cos_ref[...]
    sin = sin_ref[...]
    lane = lax.broadcasted_iota(jnp.int32, (TM, NAW), 1)
    first = (lane & 16) == 0
    q = _dot(hb, w_ref[:, 0:NAW]) * 0.125
    q_ref[0] = _rot_cols(q).astype(BF16)
    qr_ref[0] = _rot_cols(_rope(q, cos, sin, first)).astype(BF16)
    k = _dot(hb, w_ref[:, NAW:2 * NAW])
    kr_ref[0] = _rope(k, cos, sin, first).astype(BF16)
    v_ref[0] = _dot(hb, w_ref[:, 2 * NAW:3 * NAW]).astype(BF16)
    u_ref[0] = _dot(hb, w_ref[:, 3 * NAW:3 * NAW + 3 * HYW])
    g0 = 3 * NAW + 3 * HYW
    gna_ref[0] = _dot(hb, w_ref[:, g0:g0 + D]).astype(BF16)
    ghy_ref[0] = _dot(hb, w_ref[:, g0 + D:g0 + 2 * D]).astype(BF16)


def _inproj_call(x, mod3, g1, w, cos, sin, nb):
    nt = L // TM
    tok = lambda w_: pl.BlockSpec((1, TM, w_), lambda i, b: (b, i, 0))
    return pl.pallas_call(
        _inproj_kernel,
        grid=(nt, nb),
        in_specs=[tok(D),
                  pl.BlockSpec((1, 1, NMOD * D), lambda i, b: (b, 0, 0)),
                  pl.BlockSpec((1, D), lambda i, b: (0, 0)),
                  _const_spec((D, 3 * NAW + 3 * HYW + 2 * D)),
                  pl.BlockSpec((TM, NAW), lambda i, b: (i, 0)),
                  pl.BlockSpec((TM, NAW), lambda i, b: (i, 0))],
        out_specs=[tok(NAW), tok(NAW), tok(NAW), tok(NAW), tok(D), tok(D), tok(3 * HYW)],
        out_shape=[jax.ShapeDtypeStruct((nb, L, NAW), BF16)] * 4
        + [jax.ShapeDtypeStruct((nb, L, D), BF16)] * 2
        + [jax.ShapeDtypeStruct((nb, L, 3 * HYW), F32)],
        compiler_params=_cp(("arbitrary", "arbitrary")),
        name="inproj",
    )(x, mod3, g1, w, cos, sin)


def _na_geometry(r0, start):
    rows = L // GW
    geo = {}
    for ri in range(QR):
        r = r0 + ri
        ws = min(max(r - 4, 0), rows - 8)
        for u in range(BAND // 4):
            ok = [ws <= start + 4 * u + i < ws + 8 for i in range(4)]
            if not any(ok):
                geo[ri, u] = None
                continue
            lo = ok.index(True)
            hi = 4 - ok[::-1].index(True)
            base = start + 4 * u - r + 7
            geo[ri, u] = (base + 3, lo, hi)
    return geo


def _na_block(geo, start_tok, q_ref, qr_ref, k_ref, v_ref, kc_ref, vc_ref, t4_ref, o_ref):
    kc = kc_ref[0]
    vc = vc_ref[0]
    q = q_ref[0]
    nq = QR * GW
    lane_q = lax.broadcasted_iota(jnp.int32, (nq, 2 * DH), 1)
    head0 = lane_q < DH
    zero = jnp.zeros_like(q)
    q_h = (jnp.where(head0, q, zero), jnp.where(head0, zero, q))
    sc = _dot_nt(jnp.concatenate(q_h, axis=0), kc)
    half = QR * QC
    head0_t = lax.broadcasted_iota(jnp.int32, (half, 2 * DH), 1) < DH
    zero_t = jnp.zeros((half, 2 * DH), BF16)
    lane_t = lax.broadcasted_iota(jnp.int32, (QC, 2 * DH), 1)
    masks = {}
    for g in geo.values():
        if g is not None and g[1:] != (0, 4) and g[1:] not in masks:
            lo, hi = g[1:]
            masks[lo, hi] = jnp.logical_and(lane_t >= KC * lo, lane_t < KC * hi)
    pzero = jnp.zeros((QC, 2 * DH), BF16)
    for m in range(NM):
        last = m == NM - 1
        if last:
            slabs = [(GW - QC, QC), (0, QC)]
        else:
            slabs = [(QC * m, KC)]
        kt = jnp.concatenate([k_ref[0, pl.ds(start_tok + GW * bi + off, n), :]
                              for bi in range(BAND) for off, n in slabs], axis=0)
        vt = jnp.concatenate([v_ref[0, pl.ds(start_tok + GW * bi + off, n), :]
                              for bi in range(BAND) for off, n in slabs], axis=0)
        qt = jnp.concatenate([qr_ref[0, GW * ri + QC * m:GW * ri + QC * (m + 1), :]
                              for ri in range(QR)], axis=0)
        qt = jnp.concatenate([jnp.where(head0_t, qt, zero_t), jnp.where(head0_t, zero_t, qt)],
                             axis=0)
        s = _dot_nt(qt, kt)
        p_rows, pc_rows, dens = [], [], []
        for hh in range(2):
            for ri in range(QR):
                rs = QC * (QR * hh + ri)
                rc = nq * hh + GW * ri + QC * m
                sc_t = sc[rc:rc + QC]
                mvec = jnp.maximum(sc_t[:, :2 * DH], sc_t[:, 2 * DH:])
                tiles = {}
                for u in range(BAND // 4):
                    g = geo[ri, u]
                    if g is None:
                        continue
                    t = s[rs:rs + QC, 2 * DH * u:2 * DH * (u + 1)] + t4_ref[hh, int(last), g[0]]
                    if g[1:] != (0, 4):
                        t = jnp.where(masks[g[1:]], t, NEG)
                    tiles[u] = t
                    mvec = jnp.maximum(mvec, t)
                mx = jnp.max(mvec, axis=-1, keepdims=True)
                pc = jnp.exp(sc_t - mx)
                acc = pc[:, :2 * DH] + pc[:, 2 * DH:]
                prow = []
                for u in range(BAND // 4):
                    if u in tiles:
                        p = jnp.exp(tiles[u] - mx)
                        acc = acc + p
                        prow.append(p.astype(BF16))
                    else:
                        prow.append(pzero)
                p_rows.append(jnp.concatenate(prow, axis=1))
                pc_rows.append(pc.astype(BF16))
                dens.append(jnp.sum(acc, axis=-1, keepdims=True))
        o = (_dot(jnp.concatenate(p_rows, axis=0), vt)
             + _dot(jnp.concatenate(pc_rows, axis=0), vc))
        o = o / jnp.concatenate(dens, axis=0)
        ot = jnp.where(head0_t, o[:half], o[half:])
        for ri in range(QR):
            piece = ot[QC * ri:QC * (ri + 1)]
            if last:
                o_ref[0, GW * ri + GW - QSHIFT:GW * (ri + 1), :] = piece[:QSHIFT]
                o_ref[0, GW * ri:GW * ri + QSHIFT, :] = piece[QSHIFT:]
            else:
                c0 = GW * ri + QC * m + QSHIFT
                o_ref[0, c0:c0 + QC, :] = piece


def _na_kernel(q_ref, qr_ref, k_ref, v_ref, kc_ref, vc_ref, t4_ref, o_ref):
    j = pl.program_id(2)
    nj = L // (QR * GW)
    args = (q_ref, qr_ref, k_ref, v_ref, kc_ref, vc_ref, t4_ref, o_ref)

    @pl.when(j == 0)
    def _():
        _na_block(_na_geometry(0, 0), 0, *args)

    @pl.when(jnp.logical_and(j > 0, j < nj - 1))
    def _():
        start_tok = pl.multiple_of((QR * j - 4) * GW, 4 * GW)
        _na_block(_na_geometry(QR, QR - 4), start_tok, *args)

    @pl.when(j == nj - 1)
    def _():
        rows = L // GW
        _na_block(_na_geometry(rows - QR, rows - BAND), (rows - BAND) * GW, *args)


def _na_call(q, qr, kr, v, kc, vc, t4, nb):
    nq = QR * GW
    nj = L // nq
    blk = lambda: pl.BlockSpec((1, nq, 2 * DH), lambda b, hp, j: (b, j, hp))
    full = lambda n: pl.BlockSpec((1, n, 2 * DH), lambda b, hp, j: (b, 0, hp))
    return pl.pallas_call(
        _na_kernel,
        grid=(nb, NH // 2, nj),
        in_specs=[blk(), blk(), full(L), full(L), full(CTX), full(CTX),
                  pl.BlockSpec((2, 2, NT4, QC, 2 * DH), lambda b, hp, j: (hp, 0, 0, 0, 0))],
        out_specs=blk(),
        out_shape=jax.ShapeDtypeStruct((nb, L, NAW), F32),
        compiler_params=_cp(("arbitrary", "arbitrary", "arbitrary")),
        name="na",
    )(q, qr, kr, v, kc, vc, t4)


_R2 = math.sqrt(0.5)


def _bfly(e, o, k, n, sign):
    er, ei = e
    orr, oi = o
    if k == 0:
        tr, ti = orr, oi
    elif 4 * k == n:
        if sign < 0:
            return (er + oi, ei - orr), (er - oi, ei + orr)
        return (er - oi, ei + orr), (er + oi, ei - orr)
    elif 8 * k == n:
        if sign < 0:
            tr, ti = (orr + oi) * _R2, (oi - orr) * _R2
        else:
            tr, ti = (orr - oi) * _R2, (oi + orr) * _R2
    elif 8 * k == 3 * n:
        if sign < 0:
            tr, ti = (oi - orr) * _R2, (orr + oi) * (-_R2)
        else:
            tr, ti = (orr + oi) * (-_R2), (orr - oi) * _R2
    else:
        w = cmath.exp(sign * 2j * math.pi * k / n)
        tr = orr * w.real - oi * w.imag
        ti = orr * w.imag + oi * w.real
    return (er + tr, ei + ti), (er - tr, ei - ti)


def _fft(xs, sign):
    n = len(xs)
    if n == 1:
        return xs
    ev = _fft(xs[0::2], sign)
    od = _fft(xs[1::2], sign)
    out = [None] * n
    for k in range(n // 2):
        out[k], out[k + n // 2] = _bfly(ev[k], od[k], k, n, sign)
    return out


def _coarse_fwd(x_ref, a_ref, twr_ref, twi_ref, nrows, nblocks):
    def body(i, carry):
        r = pl.multiple_of(i * 8, 8)
        rows = pl.ds(r, 8)
        for c in range(NB // 128):
            xs = []
            for n1 in range(NC):
                lo = n1 * NB + c * 128
                x = x_ref[rows, lo:lo + 128]
                w = cmath.exp(-2j * math.pi * n1 / (4 * NC))
                if nblocks == 2 * NC:
                    lo2 = (n1 + NC) * NB + c * 128
                    x2 = x_ref[rows, lo2:lo2 + 128]
                    w2 = cmath.exp(-2j * math.pi * (n1 + NC) / (4 * NC))
                    xs.append((x * w.real + x2 * w2.real, x * w.imag + x2 * w2.imag))
                elif n1 == 0:
                    xs.append((x, jnp.zeros_like(x)))
                else:
                    xs.append((x * w.real, x * w.imag))
            ys = _fft(xs, -1)
            for m in range(NC):
                cs = slice(c * 128, (c + 1) * 128)
                tr = twr_ref[m, :, cs]
                ti = twi_ref[m, :, cs]
                ar, ai = ys[m]
                a_ref[m, rows, c * 128:(c + 1) * 128] = ar * tr - ai * ti
                a_ref[m, rows, NB + c * 128:NB + (c + 1) * 128] = ar * ti + ai * tr
        return carry

    lax.fori_loop(0, nrows // 8, body, 0)


def _coarse_inv(b_ref, twr_ref, twi_ref, z_ref, gate_ref, bias_ref, o_ref, nrows):
    def body(i, carry):
        r = pl.multiple_of(i * 8, 8)
        rows = pl.ds(r, 8)
        bias = bias_ref[rows, :]
        for c in range(NB // 128):
            cs = slice(c * 128, (c + 1) * 128)
            bs = []
            for m in range(NC):
                br = b_ref[m, rows, c * 128:(c + 1) * 128]
                bi = b_ref[m, rows, NB + c * 128:NB + (c + 1) * 128]
                tr = twr_ref[m, :, cs]
                ti = twi_ref[m, :, cs]
                bs.append((br * tr + bi * ti, bi * tr - br * ti))
            ys = _fft(bs, +1)
            for n1 in range(NC):
                w = cmath.exp(2j * math.pi * n1 / (4 * NC))
                yr, yi = ys[n1]
                y = yr if n1 == 0 else yr * w.real - yi * w.imag
                lo = n1 * NB + c * 128
                z = z_ref[rows, lo:lo + 128]
                o_ref[rows, lo:lo + 128] = gate_ref[rows, lo:lo + 128] * (y + bias * z)
        return carry

    lax.fori_loop(0, nrows // 8, body, 0)


def _hymlp_kernel(zt_ref, w1_ref, b1_ref, w2_ref, b2_ref, fr_ref, o_ref):
    fr = fr_ref[...]
    h1 = jnp.sin(fr * (_dot3(w1_ref[...], zt_ref[...]) + b1_ref[...]))
    o_ref[...] = jnp.sin(fr * (_dot3(w2_ref[...], h1) + b2_ref[...]))


def _hymlp_call(zt, w1t, b1, w2t, b2, fr):
    full = lambda s: pl.BlockSpec(s, lambda i: (0,) * len(s))
    return pl.pallas_call(
        _hymlp_kernel,
        grid=(1,),
        in_specs=[full(zt.shape), full(w1t.shape), full(b1.shape),
                  full(w2t.shape), full(b2.shape), full(fr.shape)],
        out_specs=full((HY_FFN, NFFT)),
        out_shape=jax.ShapeDtypeStruct((HY_FFN, NFFT), F32),
        compiler_params=_cp(("arbitrary",)),
        name="hymlp",
    )(zt, w1t, b1, w2t, b2, fr)


def _filt_kernel(h_ref, wf_ref, wb_ref, delta_ref, tpos_ref, sgn_ref, twr_ref, twi_ref,
                 fc_ref, kf_ref, kt_ref, a_ref):
    ht = h_ref[...]
    dec = jnp.exp(-delta_ref[...] * tpos_ref[...]) * sgn_ref[...]
    kt_ref[:, :L] = _dot3(wf_ref[...], ht[:, :L]) * dec[:, :L]
    kt_ref[:, L:] = _dot3(wb_ref[...], ht[:, L:]) * dec[:, L:]
    _coarse_fwd(kt_ref, a_ref, twr_ref, twi_ref, CB, 2 * NC)
    a2 = a_ref[...].reshape(NC * CB, 2 * NB)
    kf_ref[0] = (_dot3(a2, fc_ref[...]) * (2.0 / NFFT)).reshape(NC, CB, 2 * NB)


def _filt_call(hidt, w3f, w3b, delta, tpos, sgn, twr, twi, fc):
    nblk = 2 * HYW // CB
    per = HYW // CB
    return pl.pallas_call(
        _filt_kernel,
        grid=(nblk,),
        in_specs=[_const_spec((HY_FFN, NFFT)),
                  pl.BlockSpec((CB, HY_FFN), lambda i: (i, 0)),
                  pl.BlockSpec((CB, HY_FFN), lambda i: (i, 0)),
                  pl.BlockSpec((CB, 1), lambda i: (i, 0)),
                  _const_spec((1, NFFT)),
                  _const_spec((1, NFFT)),
                  _const_spec((NC, 8, NB)),
                  _const_spec((NC, 8, NB)),
                  _const_spec((2 * NB, 2 * NB))],
        out_specs=pl.BlockSpec((1, NC, CB, 2 * NB), lambda i: (i // per, 0, i % per, 0)),
        out_shape=jax.ShapeDtypeStruct((2, NC, HYW, 2 * NB), F32),
        scratch_shapes=[pltpu.VMEM((CB, NFFT), F32),
                        pltpu.VMEM((NC, CB, 2 * NB), F32)],
        compiler_params=_cp(("arbitrary",)),
        name="filt",
    )(hidt, w3f, w3b, delta, tpos, sgn, twr, twi, fc)


def _short_conv_t(u_ref, w_ref, b_ref, pad_ref, o_ref):
    pad_ref[0:8, :] = jnp.zeros((8, CB), F32)
    pad_ref[L + 8:L + 16, :] = jnp.zeros((8, CB), F32)
    pad_ref[8:L + 8, :] = u_ref[0]
    w = w_ref[...]
    ii = lax.broadcasted_iota(jnp.int32, (CB, CB), 0)
    jj = lax.broadcasted_iota(jnp.int32, (CB, CB), 1)
    eye = jnp.where(ii == jj, 1.0, 0.0).astype(BF16)
    for ch in range(L // TM):
        r = 8 + TM * ch
        s = (pad_ref[r - 1:r - 1 + TM, :] * w[0:1] + pad_ref[r:r + TM, :] * w[1:2]
             + pad_ref[r + 1:r + 1 + TM, :] * w[2:3] + b_ref[...])
        hi, lo = _split(s)
        o_ref[:, TM * ch:TM * (ch + 1)] = _dot_nt(eye, hi) + _dot_nt(eye, lo)


def _spectral_mul(x_ref, kf_ref, order, y_ref):
    for m in range(NC):
        xr = x_ref[m, :, :NB]
        xi = x_ref[m, :, NB:]
        kr = kf_ref[order, m, :, :NB]
        ki = kf_ref[order, m, :, NB:]
        y_ref[m, :, :NB] = xr * kr - xi * ki
        y_ref[m, :, NB:] = xr * ki + xi * kr


def _hyena_kernel(v_ref, x1_ref, x2_ref, cwv_ref, cw1_ref, cw2_ref, cbv_ref, cb1_ref, cb2_ref,
                  kf_ref, bias_ref, twr_ref, twi_ref, fc_ref, fi_ref, o_ref,
                  z_ref, g1_ref, g2_ref, a_ref, x_ref, pad_ref):
    _short_conv_t(v_ref, cwv_ref, cbv_ref, pad_ref, z_ref)
    _short_conv_t(x1_ref, cw1_ref, cb1_ref, pad_ref, g1_ref)
    _short_conv_t(x2_ref, cw2_ref, cb2_ref, pad_ref, g2_ref)
    for order, (gate_ref, dst_ref) in enumerate(((g1_ref, z_ref), (g2_ref, o_ref.at[0]))):
        _coarse_fwd(z_ref, a_ref, twr_ref, twi_ref, CB, NC)
        a2 = a_ref[...].reshape(NC * CB, 2 * NB).astype(BF16)
        x_ref[...] = _dot(a2, fc_ref[...]).reshape(NC, CB, 2 * NB)
        _spectral_mul(x_ref, kf_ref, order, a_ref)
        y2 = a_ref[...].reshape(NC * CB, 2 * NB).astype(BF16)
        x_ref[...] = _dot(y2, fi_ref[...]).reshape(NC, CB, 2 * NB)
        _coarse_inv(x_ref, twr_ref, twi_ref, z_ref, gate_ref, bias_ref.at[order], dst_ref, CB)


def _hyena_call(u, cw, cb, kf, bias, twr, twi, fc, fi, nb):
    per = HYW // CB
    chan = lambda part: pl.BlockSpec((1, L, CB), lambda c, b: (b, 0, part * per + c))
    roww = lambda n, part: pl.BlockSpec((n, CB), lambda c, b: (0, part * per + c))
    return pl.pallas_call(
        _hyena_kernel,
        grid=(per, nb),
        in_specs=[chan(0), chan(1), chan(2),
                  roww(3, 0), roww(3, 1), roww(3, 2), roww(1, 0), roww(1, 1), roww(1, 2),
                  pl.BlockSpec((2, NC, CB, 2 * NB), lambda c, b: (0, 0, c, 0),
                               pipeline_mode=pl.Buffered(1)),
                  pl.BlockSpec((2, CB, 128), lambda c, b: (0, c, 0)),
                  _const_spec((NC, 8, NB)),
                  _const_spec((NC, 8, NB)),
                  _const_spec((2 * NB, 2 * NB)),
                  _const_spec((2 * NB, 2 * NB))],
        out_specs=pl.BlockSpec((1, CB, L), lambda c, b: (b, c, 0)),
        out_shape=jax.ShapeDtypeStruct((nb, HYW, L), F32),
        scratch_shapes=[pltpu.VMEM((CB, L), F32),
                        pltpu.VMEM((CB, L), F32),
                        pltpu.VMEM((CB, L), F32),
                        pltpu.VMEM((NC, CB, 2 * NB), F32),
                        pltpu.VMEM((NC, CB, 2 * NB), F32),
                        pltpu.VMEM((L + 16, CB), F32)],
        compiler_params=_cp(("arbitrary", "arbitrary")),
        name="hyena",
    )(u, u, u, cw, cw, cw, cb, cb, cb, kf, bias, twr, twi, fc, fi)


FFN_CHUNKS = ((0, 1024), (1024, 2048), (2048, FFN))


def _tail_kernel(x_ref, yna_ref, yhy_ref, gna_ref, ghy_ref, mod_ref, g2_ref, gf_ref,
                 wna_ref, why_ref, wout_ref, w1_ref, w3_ref, w2_ref, o_ref):
    m = mod_ref[0]
    a = _dot(yna_ref[0].astype(BF16), wna_ref[...])
    b = _dot_tn(yhy_ref[0].astype(BF16), why_ref[...])
    mix = (_sigmoid(gna_ref[0].astype(F32)) * a + _sigmoid(ghy_ref[0].astype(F32)) * b)
    x1 = x_ref[0] + m[:, 2 * D:3 * D] * _dot(mix.astype(BF16), wout_ref[...])
    h2 = _rms_mod(x1, g2_ref[...], m[:, 3 * D:4 * D], m[:, 4 * D:5 * D]).astype(BF16)
    y = None
    for lo, hi in FFN_CHUNKS:
        p = _dot(h2, w1_ref[:, lo:hi])
        act = (p * _sigmoid(p) * _dot(h2, w3_ref[:, lo:hi])).astype(BF16)
        part = _dot(act, w2_ref[lo:hi, :])
        y = part if y is None else y + part
    x2 = x1 + m[:, 5 * D:6 * D] * y
    ms = jnp.mean(x2 * x2, axis=-1, keepdims=True)
    o_ref[0] = x2 * lax.rsqrt(ms + EPS) * gf_ref[...]


def _tail_call(x, yna, yhyt, gna, ghy, mod3, g2, gf, wna, why, wout, w1, w3, w2, nb):
    nt = L // TM
    tok = lambda w: pl.BlockSpec((1, TM, w), lambda b, i: (b, i, 0))
    row = lambda: pl.BlockSpec((1, D), lambda b, i: (0, 0))
    return pl.pallas_call(
        _tail_kernel,
        grid=(nb, nt),
        in_specs=[tok(D), tok(NAW),
                  pl.BlockSpec((1, HYW, TM), lambda b, i: (b, 0, i)),
                  tok(D), tok(D),
                  pl.BlockSpec((1, 1, NMOD * D), lambda b, i: (b, 0, 0)),
                  row(), row(),
                  _const_spec((NAW, D)), _const_spec((HYW, D)), _const_spec((D, D)),
                  _const_spec((D, FFN)), _const_spec((D, FFN)), _const_spec((FFN, D))],
        out_specs=tok(D),
        out_shape=jax.ShapeDtypeStruct((nb, L, D), F32),
        compiler_params=_cp(("arbitrary", "arbitrary")),
        name="tail",
    )(x, yna, yhyt, gna, ghy, mod3, g2, gf, wna, why, wout, w1, w3, w2)


def _rope_tables():
    t = jnp.arange(L)
    rows = (t // GW).astype(F32)
    cols = (t % GW).astype(F32)
    nf = DH // 4
    inv = THETA ** (-jnp.arange(nf, dtype=F32) / nf)
    ar = rows[:, None] * inv
    ac = cols[:, None] * inv
    cos = jnp.concatenate([jnp.cos(ar), jnp.cos(ar), jnp.cos(ac), jnp.cos(ac)], axis=-1)
    sin = jnp.concatenate([-jnp.sin(ar), jnp.sin(ar), -jnp.sin(ac), jnp.sin(ac)], axis=-1)
    return jnp.tile(cos, (1, NH)), jnp.tile(sin, (1, NH))


def _bias_tables(rpb):
    span = QC + KC
    f = jnp.pad(rpb.astype(F32), ((0, 0), (0, 0), (8, span - 8 - 31)), constant_values=NEG)
    b = jnp.tile(f, (1, 1, QC))[:, :, :QC * (span - 1)].reshape(NH, 15, QC, span - 1)
    t = b[..., QC - 1:QC - 1 + KC]
    cq = jnp.arange(QC)[:, None]
    kl = jnp.arange(KC)[None, :]
    ok_a = (kl - cq >= 0) & (kl - cq < 16)
    ok_b = jnp.where(cq < QSHIFT, kl < 16, kl >= 16)
    g = jnp.stack([jnp.where(ok_a, t, NEG), jnp.where(ok_b, t, NEG)], axis=1)
    gx = jnp.pad(g, ((0, 0), (0, 0), (3, 3), (0, 0), (0, 0)), constant_values=NEG)
    return jnp.concatenate([gx[:, :, i:i + NT4] for i in range(4)], axis=-1)


def _dft_tables():
    n2 = jnp.arange(NB)
    ang = (2.0 * math.pi / NB) * ((n2[:, None] * n2[None, :]) % NB).astype(F32)
    fr = jnp.cos(ang)
    fi = -jnp.sin(ang)
    fwd = jnp.concatenate([jnp.concatenate([fr, fi], 1), jnp.concatenate([-fi, fr], 1)], 0)
    inv = jnp.concatenate([jnp.concatenate([fr, -fi], 1), jnp.concatenate([fi, fr], 1)], 0)
    m = jnp.arange(NC)
    tang = (2.0 * math.pi / (2 * NFFT)) * ((n2[None, :] * (4 * m[:, None] + 1)) % (2 * NFFT)).astype(F32)
    twr = jnp.broadcast_to(jnp.cos(tang)[:, None, :], (NC, 8, NB))
    twi = jnp.broadcast_to(-jnp.sin(tang)[:, None, :], (NC, 8, NB))
    return fwd, inv, twr, twi


def _filter_inputs():
    t = jnp.linspace(0.0, 1.0, L, dtype=F32)[:, None]
    w = 2.0 * math.pi * jnp.arange(L, dtype=F32)[:, None] / L
    bands = jnp.linspace(1e-4, HY_BANDS - 1, HY_BANDS, dtype=F32)
    z = jnp.concatenate([t, jnp.cos(bands * w), jnp.sin(-bands * w)], axis=-1)
    zb = jnp.concatenate([z[0:1], z[:0:-1]], axis=0)
    zt = jnp.concatenate([z, zb], axis=0).T
    zt = jnp.pad(zt, ((0, HY_FFN - HY_EMB), (0, 0)))
    tl = t[:, 0]
    tpos = jnp.concatenate([tl, tl[0:1], tl[:0:-1]])[None, :]
    sgn = jnp.concatenate([jnp.ones((L,), F32), jnp.zeros((1,), F32), -jnp.ones((L - 1,), F32)])[None, :]
    min_decay = math.log(1e-2) / 1.5
    max_decay = math.log(1e-2) / 0.3
    deltas = jnp.abs(jnp.linspace(min_decay, max_decay, HYW, dtype=F32))
    return zt, tpos, sgn, deltas


def kernel(x, c, ctx, c_ctx, w_ada, b_ada, norm1_g, norm2_g, w_in, na_rpb, hy_conv_w, hy_conv_b,
           hy_ffn_w1, hy_ffn_b1, hy_ffn_w2, hy_ffn_b2, hy_sin_freq, hy_ffn_w3, hy_bias,
           w_na_o, w_hy_o, w_out, ffn_w1, ffn_w3, ffn_w2, final_g):
    nb = x.shape[0]
    assert x.shape[1:] == (L, D) and w_ada.shape[0] == 1 and nb < 16

    cvec = jnp.zeros((16, D), F32).at[:nb].set(c).at[nb].set(c_ctx)
    mod = _mod_call(cvec, w_ada[0], b_ada[0][None, :])
    mod3 = mod[:, None, :]

    wi = w_in[0]
    g1 = norm1_g[0][None, :]
    kc, vc = _ctx_call(ctx, mod3, g1, wi[:, NAW:3 * NAW].astype(BF16), nb)

    cos, sin = _rope_tables()
    q, qr, kr, v, gna, ghy, u = _inproj_call(x, mod3, g1, wi.astype(BF16), cos, sin, nb)

    yna = _na_call(q, qr, kr, v, kc, vc, _bias_tables(na_rpb[0]), nb)

    fwd, inv, twr, twi = _dft_tables()
    zt, tpos, sgn, deltas = _filter_inputs()
    col = lambda a: a.astype(F32)[:, None]
    w1t = jnp.pad(hy_ffn_w1[0].T, ((0, 0), (0, HY_FFN - HY_EMB)))
    hidt = _hymlp_call(zt, w1t, col(hy_ffn_b1[0]), hy_ffn_w2[0].T, col(hy_ffn_b2[0]),
                       col(hy_sin_freq[0]))
    w3t = hy_ffn_w3[0].T
    kf = _filt_call(hidt, w3t[:2 * HYW], w3t[2 * HYW:], jnp.tile(deltas, 2)[:, None],
                    tpos, sgn, twr, twi, fwd)

    bias = jnp.broadcast_to(hy_bias[0][:, :, None], (2, HYW, 128))
    yhyt = _hyena_call(u, hy_conv_w[0], hy_conv_b[0][None, :], kf, bias, twr, twi,
                       fwd.astype(BF16), inv.astype(BF16), nb)

    return _tail_call(x, yna, yhyt, gna, ghy, mod3, norm2_g[0][None, :], final_g[None, :],
                      w_na_o[0].astype(BF16), w_hy_o[0].astype(BF16), w_out[0].astype(BF16),
                      ffn_w1[0].astype(BF16), ffn_w3[0].astype(BF16), ffn_w2[0].astype(BF16), nb)
```

```python
import cmath
import functools
import math

import jax
import jax.numpy as jnp
from jax import lax
from jax.experimental import pallas as pl
from jax.experimental.pallas import tpu as pltpu

F32 = jnp.float32
BF16 = jnp.bfloat16

D = 1024
L = 4096
GW = 64
CTX = 256
DH = 64
NH = 8
NAW = NH * DH
HYW = 512
FFN = 2816
NMOD = 6
EPS = 1e-6
NEG = -1e30
THETA = 10000.0
HY_EMB = 33
HY_BANDS = 16
HY_FFN = 64

NFFT = 2 * L
NC = 16
NB = 256
TM = 512
CB = 128
QR = 8
BAND = 16
QC = 16
KC = 32
NM = GW // QC
QSHIFT = 8
NT4 = 18
NA_ORDER = "q0 s0 q1 s1 v0 q2 s2 v1 q3 s3 v2 v3"
VMEM_LIMIT = 56 * 1024 * 1024


def _cp(sem):
    return pltpu.CompilerParams(dimension_semantics=sem, vmem_limit_bytes=VMEM_LIMIT)


def _const_spec(shape):
    nd = len(shape)
    return pl.BlockSpec(shape, lambda *_: (0,) * nd, pipeline_mode=pl.Buffered(1))


def _dot(a, b):
    return jnp.dot(a, b, preferred_element_type=F32)


def _dot_nt(a, b):
    return lax.dot_general(a, b, (((1,), (1,)), ((), ())), preferred_element_type=F32)


def _dot_tn(a, b):
    return lax.dot_general(a, b, (((0,), (0,)), ((), ())), preferred_element_type=F32)


def _split(a):
    hi = a.astype(BF16)
    lo = (a - hi.astype(F32)).astype(BF16)
    return hi, lo


def _dot3(a, b, dot=_dot):
    ah, al = _split(a)
    bh, bl = _split(b)
    return dot(ah, bh) + dot(al, bh) + dot(ah, bl)


def _sigmoid(x):
    return 1.0 / (1.0 + jnp.exp(-x))


def _rms_mod(x, g, shift, scale):
    ms = jnp.mean(x * x, axis=-1, keepdims=True)
    y = x * lax.rsqrt(ms + EPS) * g
    return y * (1.0 + scale) + shift


def _mod_kernel(c_ref, w_ref, b_ref, o_ref):
    c = c_ref[...]
    s = c * _sigmoid(c)
    o_ref[...] = _dot3(s, w_ref[...]) + b_ref[...]


def _mod_call(cvec, w_ada, b_ada):
    nt = 4
    tn = NMOD * D // nt
    return pl.pallas_call(
        _mod_kernel,
        grid=(nt,),
        in_specs=[pl.BlockSpec((16, D), lambda i: (0, 0)),
                  pl.BlockSpec((D, tn), lambda i: (0, i)),
                  pl.BlockSpec((1, tn), lambda i: (0, i))],
        out_specs=pl.BlockSpec((16, tn), lambda i: (0, i)),
        out_shape=jax.ShapeDtypeStruct((16, NMOD * D), F32),
        compiler_params=_cp(("arbitrary",)),
        name="mod",
    )(cvec, w_ada, b_ada)


def _ctx_kernel(x_ref, mod_ref, g_ref, w_ref, k_ref, v_ref):
    m = mod_ref[0]
    h = _rms_mod(x_ref[0], g_ref[...], m[:, 0:D], m[:, D:2 * D])
    kv = _dot(h.astype(BF16), w_ref[...])
    k_ref[0] = kv[:, :NAW].astype(BF16)
    v_ref[0] = kv[:, NAW:].astype(BF16)


def _ctx_call(ctx, mod3, g1, w_kv, nb):
    return pl.pallas_call(
        _ctx_kernel,
        grid=(nb,),
        in_specs=[pl.BlockSpec((1, CTX, D), lambda b: (b, 0, 0)),
                  pl.BlockSpec((1, 1, NMOD * D), lambda b: (nb, 0, 0)),
                  pl.BlockSpec((1, D), lambda b: (0, 0)),
                  _const_spec((D, 2 * NAW))],
        out_specs=[pl.BlockSpec((1, CTX, NAW), lambda b: (b, 0, 0)),
                   pl.BlockSpec((1, CTX, NAW), lambda b: (b, 0, 0))],
        out_shape=[jax.ShapeDtypeStruct((nb, CTX, NAW), BF16)] * 2,
        compiler_params=_cp(("arbitrary",)),
        name="ctx",
    )(ctx, mod3, g1, w_kv)


def _rope(t, cos, sin_signed, first):
    up = pltpu.roll(t, NAW - 16, axis=1)
    down = pltpu.roll(t, 16, axis=1)
    return t * cos + jnp.where(first, up, down) * sin_signed


def _rot_cols(t):
    t3 = t.reshape(TM // GW, GW, t.shape[-1])
    return jnp.concatenate([t3[:, QSHIFT:], t3[:, :QSHIFT]], axis=1).reshape(t.shape)


def _inproj_kernel(x_ref, mod_ref, g_ref, w_ref, cos_ref, sin_ref,
                   q_ref, qr_ref, kr_ref, v_ref, gna_ref, ghy_ref, u_ref):
    m = mod_ref[0]
    hb = _rms_mod(x_ref[0], g_ref[...], m[:, 0:D], m[:, D:2 * D]).astype(BF16)
    cos = cos_ref[...]
    sin = sin_ref[...]
    lane = lax.broadcasted_iota(jnp.int32, (TM, NAW), 1)
    first = (lane & 16) == 0
    q = _dot(hb, w_ref[:, 0:NAW]) * 0.125
    q_ref[0] = _rot_cols(q).astype(BF16)
    qr_ref[0] = _rot_cols(_rope(q, cos, sin, first)).astype(BF16)
    k = _dot(hb, w_ref[:, NAW:2 * NAW])
    kr_ref[0] = _rope(k, cos, sin, first).astype(BF16)
    v_ref[0] = _dot(hb, w_ref[:, 2 * NAW:3 * NAW]).astype(BF16)
    u_ref[0] = _dot(hb, w_ref[:, 3 * NAW:3 * NAW + 3 * HYW])
    g0 = 3 * NAW + 3 * HYW
    gna_ref[0] = _dot(hb, w_ref[:, g0:g0 + D]).astype(BF16)
    ghy_ref[0] = _dot(hb, w_ref[:, g0 + D:g0 + 2 * D]).astype(BF16)


def _inproj_call(x, mod3, g1, w, cos, sin, nb):
    nt = L // TM
    tok = lambda w_: pl.BlockSpec((1, TM, w_), lambda i, b: (b, i, 0))
    return pl.pallas_call(
        _inproj_kernel,
        grid=(nt, nb),
        in_specs=[tok(D),
                  pl.BlockSpec((1, 1, NMOD * D), lambda i, b: (b, 0, 0)),
                  pl.BlockSpec((1, D), lambda i, b: (0, 0)),
                  _const_spec((D, 3 * NAW + 3 * HYW + 2 * D)),
                  pl.BlockSpec((TM, NAW), lambda i, b: (i, 0)),
                  pl.BlockSpec((TM, NAW), lambda i, b: (i, 0))],
        out_specs=[tok(NAW), tok(NAW), tok(NAW), tok(NAW), tok(D), tok(D), tok(3 * HYW)],
        out_shape=[jax.ShapeDtypeStruct((nb, L, NAW), BF16)] * 4
        + [jax.ShapeDtypeStruct((nb, L, D), BF16)] * 2
        + [jax.ShapeDtypeStruct((nb, L, 3 * HYW), F32)],
        compiler_params=_cp(("arbitrary", "arbitrary")),
        name="inproj",
    )(x, mod3, g1, w, cos, sin)


def _na_geometry(r0, start):
    rows = L // GW
    geo = {}
    for ri in range(QR):
        r = r0 + ri
        ws = min(max(r - 4, 0), rows - 8)
        for u in range(BAND // 4):
            ok = [ws <= start + 4 * u + i < ws + 8 for i in range(4)]
            if not any(ok):
                geo[ri, u] = None
                continue
            lo = ok.index(True)
            hi = 4 - ok[::-1].index(True)
            base = start + 4 * u - r + 7
            geo[ri, u] = (base + 3, lo, hi)
    return geo


def _na_block(geo, start_tok, q_ref, qr_ref, k_ref, v_ref, kc_ref, vc_ref, t4_ref, o_ref):
    kc = kc_ref[0]
    vc = vc_ref[0]
    q = q_ref[0]
    nq = QR * GW
    lane_q = lax.broadcasted_iota(jnp.int32, (nq, 2 * DH), 1)
    head0 = lane_q < DH
    zero = jnp.zeros_like(q)
    q_h = (jnp.where(head0, q, zero), jnp.where(head0, zero, q))
    sc = _dot_nt(jnp.concatenate(q_h, axis=0), kc)
    half = QR * QC
    head0_t = lax.broadcasted_iota(jnp.int32, (half, 2 * DH), 1) < DH
    zero_t = jnp.zeros((half, 2 * DH), BF16)
    lane_t = lax.broadcasted_iota(jnp.int32, (QC, 2 * DH), 1)
    masks = {}
    for g in geo.values():
        if g is not None and g[1:] != (0, 4) and g[1:] not in masks:
            lo, hi = g[1:]
            masks[lo, hi] = jnp.logical_and(lane_t >= KC * lo, lane_t < KC * hi)
    pzero = jnp.zeros((QC, 2 * DH), BF16)

    def tile_slabs(m):
        return [(GW - QC, QC), (0, QC)] if m == NM - 1 else [(QC * m, KC)]

    def qk(m):
        kt = jnp.concatenate([k_ref[0, pl.ds(start_tok + GW * bi + off, n), :]
                              for bi in range(BAND) for off, n in tile_slabs(m)], axis=0)
        qt = jnp.concatenate([qr_ref[0, GW * ri + QC * m:GW * ri + QC * (m + 1), :]
                              for ri in range(QR)], axis=0)
        qt = jnp.concatenate([jnp.where(head0_t, qt, zero_t), jnp.where(head0_t, zero_t, qt)],
                             axis=0)
        return _dot_nt(qt, kt)

    def softmax(m, s):
        last = m == NM - 1
        p_rows, pc_rows, dens = [], [], []
        for hh in range(2):
            for ri in range(QR):
                rs = QC * (QR * hh + ri)
                rc = nq * hh + GW * ri + QC * m
                sc_t = sc[rc:rc + QC]
                mvec = jnp.maximum(sc_t[:, :2 * DH], sc_t[:, 2 * DH:])
                tiles = {}
                for u in range(BAND // 4):
                    g = geo[ri, u]
                    if g is None:
                        continue
                    t = s[rs:rs + QC, 2 * DH * u:2 * DH * (u + 1)] + t4_ref[hh, int(last), g[0]]
                    if g[1:] != (0, 4):
                        t = jnp.where(masks[g[1:]], t, NEG)
                    tiles[u] = t
                    mvec = jnp.maximum(mvec, t)
                mx = jnp.max(mvec, axis=-1, keepdims=True)
                pc = jnp.exp(sc_t - mx)
                acc = pc[:, :2 * DH] + pc[:, 2 * DH:]
                prow = []
                for u in range(BAND // 4):
                    if u in tiles:
                        p = jnp.exp(tiles[u] - mx)
                        acc = acc + p
                        prow.append(p.astype(BF16))
                    else:
                        prow.append(pzero)
                p_rows.append(jnp.concatenate(prow, axis=1))
                pc_rows.append(pc.astype(BF16))
                dens.append(jnp.sum(acc, axis=-1, keepdims=True))
        return (jnp.concatenate(p_rows, axis=0), jnp.concatenate(pc_rows, axis=0),
                jnp.concatenate(dens, axis=0))

    def pv(m, p, pc, den):
        vt = jnp.concatenate([v_ref[0, pl.ds(start_tok + GW * bi + off, n), :]
                              for bi in range(BAND) for off, n in tile_slabs(m)], axis=0)
        o = (_dot(p, vt) + _dot(pc, vc)) / den
        ot = jnp.where(head0_t, o[:half], o[half:])
        for ri in range(QR):
            piece = ot[QC * ri:QC * (ri + 1)]
            if m == NM - 1:
                o_ref[0, GW * ri + GW - QSHIFT:GW * (ri + 1), :] = piece[:QSHIFT]
                o_ref[0, GW * ri:GW * ri + QSHIFT, :] = piece[QSHIFT:]
            else:
                c0 = GW * ri + QC * m + QSHIFT
                o_ref[0, c0:c0 + QC, :] = piece

    scores, probs = {}, {}
    for step in NA_ORDER.split():
        m = int(step[1])
        if step[0] == "q":
            scores[m] = qk(m)
        elif step[0] == "s":
            probs[m] = softmax(m, scores.pop(m))
        else:
            pv(m, *probs.pop(m))


def _na_kernel(q_ref, qr_ref, k_ref, v_ref, kc_ref, vc_ref, t4_ref, o_ref):
    j = pl.program_id(2)
    nj = L // (QR * GW)
    args = (q_ref, qr_ref, k_ref, v_ref, kc_ref, vc_ref, t4_ref, o_ref)

    @pl.when(j == 0)
    def _():
        _na_block(_na_geometry(0, 0), 0, *args)

    @pl.when(jnp.logical_and(j > 0, j < nj - 1))
    def _():
        start_tok = pl.multiple_of((QR * j - 4) * GW, 4 * GW)
        _na_block(_na_geometry(QR, QR - 4), start_tok, *args)

    @pl.when(j == nj - 1)
    def _():
        rows = L // GW
        _na_block(_na_geometry(rows - QR, rows - BAND), (rows - BAND) * GW, *args)


def _na_call(q, qr, kr, v, kc, vc, t4, nb):
    nq = QR * GW
    nj = L // nq
    blk = lambda: pl.BlockSpec((1, nq, 2 * DH), lambda b, hp, j: (b, j, hp))
    full = lambda n: pl.BlockSpec((1, n, 2 * DH), lambda b, hp, j: (b, 0, hp))
    return pl.pallas_call(
        _na_kernel,
        grid=(nb, NH // 2, nj),
        in_specs=[blk(), blk(), full(L), full(L), full(CTX), full(CTX),
                  pl.BlockSpec((2, 2, NT4, QC, 2 * DH), lambda b, hp, j: (hp, 0, 0, 0, 0))],
        out_specs=blk(),
        out_shape=jax.ShapeDtypeStruct((nb, L, NAW), F32),
        compiler_params=_cp(("arbitrary", "arbitrary", "arbitrary")),
        name="na",
    )(q, qr, kr, v, kc, vc, t4)


_R2 = math.sqrt(0.5)


def _bfly(e, o, k, n, sign):
    er, ei = e
    orr, oi = o
    if k == 0:
        tr, ti = orr, oi
    elif 4 * k == n:
        if sign < 0:
            return (er + oi, ei - orr), (er - oi, ei + orr)
        return (er - oi, ei + orr), (er + oi, ei - orr)
    elif 8 * k == n:
        if sign < 0:
            tr, ti = (orr + oi) * _R2, (oi - orr) * _R2
        else:
            tr, ti = (orr - oi) * _R2, (oi + orr) * _R2
    elif 8 * k == 3 * n:
        if sign < 0:
            tr, ti = (oi - orr) * _R2, (orr + oi) * (-_R2)
        else:
            tr, ti = (orr + oi) * (-_R2), (orr - oi) * _R2
    else:
        w = cmath.exp(sign * 2j * math.pi * k / n)
        tr = orr * w.real - oi * w.imag
        ti = orr * w.imag + oi * w.real
    return (er + tr, ei + ti), (er - tr, ei - ti)


def _fft(xs, sign):
    n = len(xs)
    if n == 1:
        return xs
    ev = _fft(xs[0::2], sign)
    od = _fft(xs[1::2], sign)
    out = [None] * n
    for k in range(n // 2):
        out[k], out[k + n // 2] = _bfly(ev[k], od[k], k, n, sign)
    return out


def _for_row_tiles(row0, nrows, unrolled, body):
    if unrolled:
        for r in range(row0, row0 + nrows, 8):
            body(pl.ds(r, 8))
    else:
        def step(i, carry):
            body(pl.ds(pl.multiple_of(row0 + i * 8, 8), 8))
            return carry
        lax.fori_loop(0, nrows // 8, step, 0)


def _coarse_fwd(x_ref, a_ref, twr_ref, twi_ref, row0, nrows, nblocks, unrolled=False):
    def body(rows):
        for c in range(NB // 128):
            xs = []
            for n1 in range(NC):
                lo = n1 * NB + c * 128
                x = x_ref[rows, lo:lo + 128]
                w = cmath.exp(-2j * math.pi * n1 / (4 * NC))
                if nblocks == 2 * NC:
                    lo2 = (n1 + NC) * NB + c * 128
                    x2 = x_ref[rows, lo2:lo2 + 128]
                    w2 = cmath.exp(-2j * math.pi * (n1 + NC) / (4 * NC))
                    xs.append((x * w.real + x2 * w2.real, x * w.imag + x2 * w2.imag))
                elif n1 == 0:
                    xs.append((x, jnp.zeros_like(x)))
                else:
                    xs.append((x * w.real, x * w.imag))
            ys = _fft(xs, -1)
            for m in range(NC):
                cs = slice(c * 128, (c + 1) * 128)
                tr = twr_ref[m, :, cs]
                ti = twi_ref[m, :, cs]
                ar, ai = ys[m]
                a_ref[m, rows, c * 128:(c + 1) * 128] = ar * tr - ai * ti
                a_ref[m, rows, NB + c * 128:NB + (c + 1) * 128] = ar * ti + ai * tr

    _for_row_tiles(row0, nrows, unrolled, body)


def _coarse_inv(b_ref, twr_ref, twi_ref, z_ref, gate_ref, bias_ref, o_ref, row0, nrows,
                unrolled=False):
    def body(rows):
        bias = bias_ref[rows, :]
        for c in range(NB // 128):
            cs = slice(c * 128, (c + 1) * 128)
            bs = []
            for m in range(NC):
                br = b_ref[m, rows, c * 128:(c + 1) * 128]
                bi = b_ref[m, rows, NB + c * 128:NB + (c + 1) * 128]
                tr = twr_ref[m, :, cs]
                ti = twi_ref[m, :, cs]
                bs.append((br * tr + bi * ti, bi * tr - br * ti))
            ys = _fft(bs, +1)
            for n1 in range(NC):
                w = cmath.exp(2j * math.pi * n1 / (4 * NC))
                yr, yi = ys[n1]
                y = yr if n1 == 0 else yr * w.real - yi * w.imag
                lo = n1 * NB + c * 128
                z = z_ref[rows, lo:lo + 128]
                o_ref[rows, lo:lo + 128] = gate_ref[rows, lo:lo + 128] * (y + bias * z)

    _for_row_tiles(row0, nrows, unrolled, body)


def _hymlp_kernel(zt_ref, w1_ref, b1_ref, w2_ref, b2_ref, fr_ref, o_ref):
    fr = fr_ref[...]
    h1 = jnp.sin(fr * (_dot3(w1_ref[...], zt_ref[...]) + b1_ref[...]))
    o_ref[...] = jnp.sin(fr * (_dot3(w2_ref[...], h1) + b2_ref[...]))


def _hymlp_call(zt, w1t, b1, w2t, b2, fr):
    full = lambda s: pl.BlockSpec(s, lambda i: (0,) * len(s))
    return pl.pallas_call(
        _hymlp_kernel,
        grid=(1,),
        in_specs=[full(zt.shape), full(w1t.shape), full(b1.shape),
                  full(w2t.shape), full(b2.shape), full(fr.shape)],
        out_specs=full((HY_FFN, NFFT)),
        out_shape=jax.ShapeDtypeStruct((HY_FFN, NFFT), F32),
        compiler_params=_cp(("arbitrary",)),
        name="hymlp",
    )(zt, w1t, b1, w2t, b2, fr)


def _filt_kernel(h_ref, wf_ref, wb_ref, delta_ref, tpos_ref, sgn_ref, twr_ref, twi_ref,
                 fc_ref, kf_ref, kt_ref, a_ref):
    ht = h_ref[...]
    dec = jnp.exp(-delta_ref[...] * tpos_ref[...]) * sgn_ref[...]
    kt_ref[:, :L] = _dot3(wf_ref[...], ht[:, :L]) * dec[:, :L]
    kt_ref[:, L:] = _dot3(wb_ref[...], ht[:, L:]) * dec[:, L:]
    _coarse_fwd(kt_ref, a_ref, twr_ref, twi_ref, 0, CB, 2 * NC)
    a2 = a_ref[...].reshape(NC * CB, 2 * NB)
    kf_ref[0] = (_dot3(a2, fc_ref[...]) * (2.0 / NFFT)).reshape(NC, CB, 2 * NB)


def _filt_call(hidt, w3f, w3b, delta, tpos, sgn, twr, twi, fc):
    nblk = 2 * HYW // CB
    per = HYW // CB
    return pl.pallas_call(
        _filt_kernel,
        grid=(nblk,),
        in_specs=[_const_spec((HY_FFN, NFFT)),
                  pl.BlockSpec((CB, HY_FFN), lambda i: (i, 0)),
                  pl.BlockSpec((CB, HY_FFN), lambda i: (i, 0)),
                  pl.BlockSpec((CB, 1), lambda i: (i, 0)),
                  _const_spec((1, NFFT)),
                  _const_spec((1, NFFT)),
                  _const_spec((NC, 8, NB)),
                  _const_spec((NC, 8, NB)),
                  _const_spec((2 * NB, 2 * NB))],
        out_specs=pl.BlockSpec((1, NC, CB, 2 * NB), lambda i: (i // per, 0, i % per, 0)),
        out_shape=jax.ShapeDtypeStruct((2, NC, HYW, 2 * NB), F32),
        scratch_shapes=[pltpu.VMEM((CB, NFFT), F32),
                        pltpu.VMEM((NC, CB, 2 * NB), F32)],
        compiler_params=_cp(("arbitrary",)),
        name="filt",
    )(hidt, w3f, w3b, delta, tpos, sgn, twr, twi, fc)


def _short_conv_t(u_ref, w_ref, b_ref, pad_ref, o_ref):
    pad_ref[0:8, :] = jnp.zeros((8, CB), F32)
    pad_ref[L + 8:L + 16, :] = jnp.zeros((8, CB), F32)
    pad_ref[8:L + 8, :] = u_ref[0]
    w = w_ref[...]
    ii = lax.broadcasted_iota(jnp.int32, (CB, CB), 0)
    jj = lax.broadcasted_iota(jnp.int32, (CB, CB), 1)
    eye = jnp.where(ii == jj, 1.0, 0.0).astype(BF16)
    for ch in range(L // TM):
        r = 8 + TM * ch
        s = (pad_ref[r - 1:r - 1 + TM, :] * w[0:1] + pad_ref[r:r + TM, :] * w[1:2]
             + pad_ref[r + 1:r + 1 + TM, :] * w[2:3] + b_ref[...])
        hi, lo = _split(s)
        o_ref[:, TM * ch:TM * (ch + 1)] = _dot_nt(eye, hi) + _dot_nt(eye, lo)


def _spectral_mul(x_ref, kf_ref, order, y_ref, rows):
    for m in range(NC):
        xr = x_ref[m, rows, :NB]
        xi = x_ref[m, rows, NB:]
        kr = kf_ref[order, m, rows, :NB]
        ki = kf_ref[order, m, rows, NB:]
        y_ref[m, rows, :NB] = xr * kr - xi * ki
        y_ref[m, rows, NB:] = xr * ki + xi * kr


HYENA_ORDER = (
    ("fwd", 0, 0), ("dft", 0, 0), ("fwd", 0, 1), ("dft", 0, 1),
    ("mul", 0, 0), ("idft", 0, 0), ("mul", 0, 1), ("idft", 0, 1),
    ("inv", 0, 0), ("fwd", 1, 0), ("dft", 1, 0), ("inv", 0, 1), ("fwd", 1, 1), ("dft", 1, 1),
    ("mul", 1, 0), ("idft", 1, 0), ("mul", 1, 1), ("idft", 1, 1),
    ("inv", 1, 0), ("inv", 1, 1),
)


def _hyena_kernel(v_ref, x1_ref, x2_ref, cwv_ref, cw1_ref, cw2_ref, cbv_ref, cb1_ref, cb2_ref,
                  kf_ref, bias_ref, twr_ref, twi_ref, fc_ref, fi_ref, o_ref,
                  z_ref, g1_ref, g2_ref, a_ref, x_ref, pad_ref):
    _short_conv_t(v_ref, cwv_ref, cbv_ref, pad_ref, z_ref)
    _short_conv_t(x1_ref, cw1_ref, cb1_ref, pad_ref, g1_ref)
    _short_conv_t(x2_ref, cw2_ref, cb2_ref, pad_ref, g2_ref)
    hb = CB // 2

    def matmul(rows, w_ref):
        a2 = a_ref[:, rows, :].reshape(NC * hb, 2 * NB).astype(BF16)
        x_ref[:, rows, :] = _dot(a2, w_ref[...]).reshape(NC, hb, 2 * NB)

    def stage(name, order, half):
        row0 = half * hb
        rows = slice(row0, row0 + hb)
        gate_ref, dst_ref = ((g1_ref, z_ref), (g2_ref, o_ref.at[0]))[order]
        if name == "fwd":
            _coarse_fwd(z_ref, a_ref, twr_ref, twi_ref, row0, hb, NC, unrolled=True)
        elif name == "dft":
            matmul(rows, fc_ref)
        elif name == "mul":
            _spectral_mul(x_ref, kf_ref, order, a_ref, rows)
        elif name == "idft":
            matmul(rows, fi_ref)
        else:
            _coarse_inv(x_ref, twr_ref, twi_ref, z_ref, gate_ref, bias_ref.at[order], dst_ref,
                        row0, hb, unrolled=True)

    for name, order, half in HYENA_ORDER:
        stage(name, order, half)


def _hyena_call(u, cw, cb, kf, bias, twr, twi, fc, fi, nb):
    per = HYW // CB
    chan = lambda part: pl.BlockSpec((1, L, CB), lambda c, b: (b, 0, part * per + c))
    roww = lambda n, part: pl.BlockSpec((n, CB), lambda c, b: (0, part * per + c))
    return pl.pallas_call(
        _hyena_kernel,
        grid=(per, nb),
        in_specs=[chan(0), chan(1), chan(2),
                  roww(3, 0), roww(3, 1), roww(3, 2), roww(1, 0), roww(1, 1), roww(1, 2),
                  pl.BlockSpec((2, NC, CB, 2 * NB), lambda c, b: (0, 0, c, 0),
                               pipeline_mode=pl.Buffered(1)),
                  pl.BlockSpec((2, CB, 128), lambda c, b: (0, c, 0)),
                  _const_spec((NC, 8, NB)),
                  _const_spec((NC, 8, NB)),
                  _const_spec((2 * NB, 2 * NB)),
                  _const_spec((2 * NB, 2 * NB))],
        out_specs=pl.BlockSpec((1, CB, L), lambda c, b: (b, c, 0)),
        out_shape=jax.ShapeDtypeStruct((nb, HYW, L), F32),
        scratch_shapes=[pltpu.VMEM((CB, L), F32),
                        pltpu.VMEM((CB, L), F32),
                        pltpu.VMEM((CB, L), F32),
                        pltpu.VMEM((NC, CB, 2 * NB), F32),
                        pltpu.VMEM((NC, CB, 2 * NB), F32),
                        pltpu.VMEM((L + 16, CB), F32)],
        compiler_params=_cp(("arbitrary", "arbitrary")),
        name="hyena",
    )(u, u, u, cw, cw, cw, cb, cb, cb, kf, bias, twr, twi, fc, fi)


FFN_CHUNKS = ((0, 1024), (1024, 2048), (2048, FFN))


def _tail_kernel(x_ref, yna_ref, yhy_ref, gna_ref, ghy_ref, mod_ref, g2_ref, gf_ref,
                 wna_ref, why_ref, wout_ref, w1_ref, w3_ref, w2_ref, o_ref):
    m = mod_ref[0]
    a = _dot(yna_ref[0].astype(BF16), wna_ref[...])
    b = _dot_tn(yhy_ref[0].astype(BF16), why_ref[...])
    mix = (_sigmoid(gna_ref[0].astype(F32)) * a + _sigmoid(ghy_ref[0].astype(F32)) * b)
    x1 = x_ref[0] + m[:, 2 * D:3 * D] * _dot(mix.astype(BF16), wout_ref[...])
    h2 = _rms_mod(x1, g2_ref[...], m[:, 3 * D:4 * D], m[:, 4 * D:5 * D]).astype(BF16)
    y = None
    for lo, hi in FFN_CHUNKS:
        p = _dot(h2, w1_ref[:, lo:hi])
        act = (p * _sigmoid(p) * _dot(h2, w3_ref[:, lo:hi])).astype(BF16)
        part = _dot(act, w2_ref[lo:hi, :])
        y = part if y is None else y + part
    x2 = x1 + m[:, 5 * D:6 * D] * y
    ms = jnp.mean(x2 * x2, axis=-1, keepdims=True)
    o_ref[0] = x2 * lax.rsqrt(ms + EPS) * gf_ref[...]


def _tail_call(x, yna, yhyt, gna, ghy, mod3, g2, gf, wna, why, wout, w1, w3, w2, nb):
    nt = L // TM
    tok = lambda w: pl.BlockSpec((1, TM, w), lambda b, i: (b, i, 0))
    row = lambda: pl.BlockSpec((1, D), lambda b, i: (0, 0))
    return pl.pallas_call(
        _tail_kernel,
        grid=(nb, nt),
        in_specs=[tok(D), tok(NAW),
                  pl.BlockSpec((1, HYW, TM), lambda b, i: (b, 0, i)),
                  tok(D), tok(D),
                  pl.BlockSpec((1, 1, NMOD * D), lambda b, i: (b, 0, 0)),
                  row(), row(),
                  _const_spec((NAW, D)), _const_spec((HYW, D)), _const_spec((D, D)),
                  _const_spec((D, FFN)), _const_spec((D, FFN)), _const_spec((FFN, D))],
        out_specs=tok(D),
        out_shape=jax.ShapeDtypeStruct((nb, L, D), F32),
        compiler_params=_cp(("arbitrary", "arbitrary")),
        name="tail",
    )(x, yna, yhyt, gna, ghy, mod3, g2, gf, wna, why, wout, w1, w3, w2)


def _rope_tables():
    t = jnp.arange(L)
    rows = (t // GW).astype(F32)
    cols = (t % GW).astype(F32)
    nf = DH // 4
    inv = THETA ** (-jnp.arange(nf, dtype=F32) / nf)
    ar = rows[:, None] * inv
    ac = cols[:, None] * inv
    cos = jnp.concatenate([jnp.cos(ar), jnp.cos(ar), jnp.cos(ac), jnp.cos(ac)], axis=-1)
    sin = jnp.concatenate([-jnp.sin(ar), jnp.sin(ar), -jnp.sin(ac), jnp.sin(ac)], axis=-1)
    return jnp.tile(cos, (1, NH)), jnp.tile(sin, (1, NH))


def _bias_tables(rpb):
    span = QC + KC
    f = jnp.pad(rpb.astype(F32), ((0, 0), (0, 0), (8, span - 8 - 31)), constant_values=NEG)
    b = jnp.tile(f, (1, 1, QC))[:, :, :QC * (span - 1)].reshape(NH, 15, QC, span - 1)
    t = b[..., QC - 1:QC - 1 + KC]
    cq = jnp.arange(QC)[:, None]
    kl = jnp.arange(KC)[None, :]
    ok_a = (kl - cq >= 0) & (kl - cq < 16)
    ok_b = jnp.where(cq < QSHIFT, kl < 16, kl >= 16)
    g = jnp.stack([jnp.where(ok_a, t, NEG), jnp.where(ok_b, t, NEG)], axis=1)
    gx = jnp.pad(g, ((0, 0), (0, 0), (3, 3), (0, 0), (0, 0)), constant_values=NEG)
    return jnp.concatenate([gx[:, :, i:i + NT4] for i in range(4)], axis=-1)


def _dft_tables():
    n2 = jnp.arange(NB)
    ang = (2.0 * math.pi / NB) * ((n2[:, None] * n2[None, :]) % NB).astype(F32)
    fr = jnp.cos(ang)
    fi = -jnp.sin(ang)
    fwd = jnp.concatenate([jnp.concatenate([fr, fi], 1), jnp.concatenate([-fi, fr], 1)], 0)
    inv = jnp.concatenate([jnp.concatenate([fr, -fi], 1), jnp.concatenate([fi, fr], 1)], 0)
    m = jnp.arange(NC)
    tang = (2.0 * math.pi / (2 * NFFT)) * ((n2[None, :] * (4 * m[:, None] + 1)) % (2 * NFFT)).astype(F32)
    twr = jnp.broadcast_to(jnp.cos(tang)[:, None, :], (NC, 8, NB))
    twi = jnp.broadcast_to(-jnp.sin(tang)[:, None, :], (NC, 8, NB))
    return fwd, inv, twr, twi


def _filter_inputs():
    t = jnp.linspace(0.0, 1.0, L, dtype=F32)[:, None]
    w = 2.0 * math.pi * jnp.arange(L, dtype=F32)[:, None] / L
    bands = jnp.linspace(1e-4, HY_BANDS - 1, HY_BANDS, dtype=F32)
    z = jnp.concatenate([t, jnp.cos(bands * w), jnp.sin(-bands * w)], axis=-1)
    zb = jnp.concatenate([z[0:1], z[:0:-1]], axis=0)
    zt = jnp.concatenate([z, zb], axis=0).T
    zt = jnp.pad(zt, ((0, HY_FFN - HY_EMB), (0, 0)))
    tl = t[:, 0]
    tpos = jnp.concatenate([tl, tl[0:1], tl[:0:-1]])[None, :]
    sgn = jnp.concatenate([jnp.ones((L,), F32), jnp.zeros((1,), F32), -jnp.ones((L - 1,), F32)])[None, :]
    min_decay = math.log(1e-2) / 1.5
    max_decay = math.log(1e-2) / 0.3
    deltas = jnp.abs(jnp.linspace(min_decay, max_decay, HYW, dtype=F32))
    return zt, tpos, sgn, deltas


def kernel(x, c, ctx, c_ctx, w_ada, b_ada, norm1_g, norm2_g, w_in, na_rpb, hy_conv_w, hy_conv_b,
           hy_ffn_w1, hy_ffn_b1, hy_ffn_w2, hy_ffn_b2, hy_sin_freq, hy_ffn_w3, hy_bias,
           w_na_o, w_hy_o, w_out, ffn_w1, ffn_w3, ffn_w2, final_g):
    nb = x.shape[0]
    assert x.shape[1:] == (L, D) and w_ada.shape[0] == 1 and nb < 16

    cvec = jnp.zeros((16, D), F32).at[:nb].set(c).at[nb].set(c_ctx)
    mod = _mod_call(cvec, w_ada[0], b_ada[0][None, :])
    mod3 = mod[:, None, :]

    wi = w_in[0]
    g1 = norm1_g[0][None, :]
    kc, vc = _ctx_call(ctx, mod3, g1, wi[:, NAW:3 * NAW].astype(BF16), nb)

    cos, sin = _rope_tables()
    q, qr, kr, v, gna, ghy, u = _inproj_call(x, mod3, g1, wi.astype(BF16), cos, sin, nb)

    yna = _na_call(q, qr, kr, v, kc, vc, _bias_tables(na_rpb[0]), nb)

    fwd, inv, twr, twi = _dft_tables()
    zt, tpos, sgn, deltas = _filter_inputs()
    col = lambda a: a.astype(F32)[:, None]
    w1t = jnp.pad(hy_ffn_w1[0].T, ((0, 0), (0, HY_FFN - HY_EMB)))
    hidt = _hymlp_call(zt, w1t, col(hy_ffn_b1[0]), hy_ffn_w2[0].T, col(hy_ffn_b2[0]),
                       col(hy_sin_freq[0]))
    w3t = hy_ffn_w3[0].T
    kf = _filt_call(hidt, w3t[:2 * HYW], w3t[2 * HYW:], jnp.tile(deltas, 2)[:, None],
                    tpos, sgn, twr, twi, fwd)

    bias = jnp.broadcast_to(hy_bias[0][:, :, None], (2, HYW, 128))
    yhyt = _hyena_call(u, hy_conv_w[0], hy_conv_b[0][None, :], kf, bias, twr, twi,
                       fwd.astype(BF16), inv.astype(BF16), nb)

    return _tail_call(x, yna, yhyt, gna, ghy, mod3, norm2_g[0][None, :], final_g[None, :],
                      w_na_o[0].astype(BF16), w_hy_o[0].astype(BF16), w_out[0].astype(BF16),
                      ffn_w1[0].astype(BF16), ffn_w3[0].astype(BF16), ffn_w2[0].astype(BF16), nb)
```

```python
import cmath
import functools
import math

import jax
import jax.numpy as jnp
from jax import lax
from jax.experimental import pallas as pl
from jax.experimental.pallas import tpu as pltpu

F32 = jnp.float32
BF16 = jnp.bfloat16

D = 1024
L = 4096
GW = 64
CTX = 256
DH = 64
NH = 8
NAW = NH * DH
HYW = 512
FFN = 2816
NMOD = 6
EPS = 1e-6
NEG = -1e30
THETA = 10000.0
HY_EMB = 33
HY_BANDS = 16
HY_FFN = 64

NFFT = 2 * L
NC = 16
NB = 256
TM = 512
CB = 128
QR = 8
BAND = 16
QC = 16
KC = 32
NM = GW // QC
QSHIFT = 8
NT4 = 18
NA_ORDER = "q0 s0 q1 s1 v0 q2 s2 v1 q3 s3 v2 v3"
VMEM_LIMIT = 56 * 1024 * 1024


def _cp(sem):
    return pltpu.CompilerParams(dimension_semantics=sem, vmem_limit_bytes=VMEM_LIMIT)


def _const_spec(shape):
    nd = len(shape)
    return pl.BlockSpec(shape, lambda *_: (0,) * nd, pipeline_mode=pl.Buffered(1))


def _dot(a, b):
    return jnp.dot(a, b, preferred_element_type=F32)


def _dot_nt(a, b):
    return lax.dot_general(a, b, (((1,), (1,)), ((), ())), preferred_element_type=F32)


def _dot_tn(a, b):
    return lax.dot_general(a, b, (((0,), (0,)), ((), ())), preferred_element_type=F32)


def _split(a):
    hi = a.astype(BF16)
    lo = (a - hi.astype(F32)).astype(BF16)
    return hi, lo


def _dot3(a, b, dot=_dot):
    ah, al = _split(a)
    bh, bl = _split(b)
    return dot(ah, bh) + dot(al, bh) + dot(ah, bl)


def _sigmoid(x):
    return 1.0 / (1.0 + jnp.exp(-x))


def _rms_mod(x, g, shift, scale):
    ms = jnp.mean(x * x, axis=-1, keepdims=True)
    y = x * lax.rsqrt(ms + EPS) * g
    return y * (1.0 + scale) + shift


def _mod_kernel(c_ref, w_ref, b_ref, o_ref):
    c = c_ref[...]
    s = c * _sigmoid(c)
    o_ref[...] = _dot3(s, w_ref[...]) + b_ref[...]


def _mod_call(cvec, w_ada, b_ada):
    nt = 4
    tn = NMOD * D // nt
    return pl.pallas_call(
        _mod_kernel,
        grid=(nt,),
        in_specs=[pl.BlockSpec((16, D), lambda i: (0, 0)),
                  pl.BlockSpec((D, tn), lambda i: (0, i)),
                  pl.BlockSpec((1, tn), lambda i: (0, i))],
        out_specs=pl.BlockSpec((16, tn), lambda i: (0, i)),
        out_shape=jax.ShapeDtypeStruct((16, NMOD * D), F32),
        compiler_params=_cp(("arbitrary",)),
        name="mod",
    )(cvec, w_ada, b_ada)


def _ctx_kernel(x_ref, mod_ref, g_ref, w_ref, k_ref, v_ref):
    m = mod_ref[0]
    h = _rms_mod(x_ref[0], g_ref[...], m[:, 0:D], m[:, D:2 * D])
    kv = _dot(h.astype(BF16), w_ref[...])
    k_ref[0] = kv[:, :NAW].astype(BF16)
    v_ref[0] = kv[:, NAW:].astype(BF16)


def _ctx_call(ctx, mod3, g1, w_kv, nb):
    return pl.pallas_call(
        _ctx_kernel,
        grid=(nb,),
        in_specs=[pl.BlockSpec((1, CTX, D), lambda b: (b, 0, 0)),
                  pl.BlockSpec((1, 1, NMOD * D), lambda b: (nb, 0, 0)),
                  pl.BlockSpec((1, D), lambda b: (0, 0)),
                  _const_spec((D, 2 * NAW))],
        out_specs=[pl.BlockSpec((1, CTX, NAW), lambda b: (b, 0, 0)),
                   pl.BlockSpec((1, CTX, NAW), lambda b: (b, 0, 0))],
        out_shape=[jax.ShapeDtypeStruct((nb, CTX, NAW), BF16)] * 2,
        compiler_params=_cp(("arbitrary",)),
        name="ctx",
    )(ctx, mod3, g1, w_kv)


def _rope(t, cos, sin_signed, first):
    up = pltpu.roll(t, NAW - 16, axis=1)
    down = pltpu.roll(t, 16, axis=1)
    return t * cos + jnp.where(first, up, down) * sin_signed


def _rot_cols(t):
    t3 = t.reshape(TM // GW, GW, t.shape[-1])
    return jnp.concatenate([t3[:, QSHIFT:], t3[:, :QSHIFT]], axis=1).reshape(t.shape)


def _inproj_kernel(x_ref, mod_ref, g_ref, w_ref, cos_ref, sin_ref,
                   q_ref, qr_ref, kr_ref, v_ref, gna_ref, ghy_ref, u_ref):
    m = mod_ref[0]
    hb = _rms_mod(x_ref[0], g_ref[...], m[:, 0:D], m[:, D:2 * D]).astype(BF16)
    cos = cos_ref[...]
    sin = sin_ref[...]
    lane = lax.broadcasted_iota(jnp.int32, (TM, NAW), 1)
    first = (lane & 16) == 0
    q = _dot(hb, w_ref[:, 0:NAW]) * 0.125
    q_ref[0] = _rot_cols(q).astype(BF16)
    qr_ref[0] = _rot_cols(_rope(q, cos, sin, first)).astype(BF16)
    k = _dot(hb, w_ref[:, NAW:2 * NAW])
    kr_ref[0] = _rope(k, cos, sin, first).astype(BF16)
    v_ref[0] = _dot(hb, w_ref[:, 2 * NAW:3 * NAW]).astype(BF16)
    u_ref[0] = _dot(hb, w_ref[:, 3 * NAW:3 * NAW + 3 * HYW])
    g0 = 3 * NAW + 3 * HYW
    gna_ref[0] = _dot(hb, w_ref[:, g0:g0 + D]).astype(BF16)
    ghy_ref[0] = _dot(hb, w_ref[:, g0 + D:g0 + 2 * D]).astype(BF16)


def _inproj_call(x, mod3, g1, w, cos, sin, nb):
    nt = L // TM
    tok = lambda w_: pl.BlockSpec((1, TM, w_), lambda i, b: (b, i, 0))
    return pl.pallas_call(
        _inproj_kernel,
        grid=(nt, nb),
        in_specs=[tok(D),
                  pl.BlockSpec((1, 1, NMOD * D), lambda i, b: (b, 0, 0)),
                  pl.BlockSpec((1, D), lambda i, b: (0, 0)),
                  _const_spec((D, 3 * NAW + 3 * HYW + 2 * D)),
                  pl.BlockSpec((TM, NAW), lambda i, b: (i, 0)),
                  pl.BlockSpec((TM, NAW), lambda i, b: (i, 0))],
        out_specs=[tok(NAW), tok(NAW), tok(NAW), tok(NAW), tok(D), tok(D), tok(3 * HYW)],
        out_shape=[jax.ShapeDtypeStruct((nb, L, NAW), BF16)] * 4
        + [jax.ShapeDtypeStruct((nb, L, D), BF16)] * 2
        + [jax.ShapeDtypeStruct((nb, L, 3 * HYW), F32)],
        compiler_params=_cp(("arbitrary", "arbitrary")),
        name="inproj",
    )(x, mod3, g1, w, cos, sin)


def _na_geometry(r0, start):
    rows = L // GW
    geo = {}
    for ri in range(QR):
        r = r0 + ri
        ws = min(max(r - 4, 0), rows - 8)
        for u in range(BAND // 4):
            ok = [ws <= start + 4 * u + i < ws + 8 for i in range(4)]
            if not any(ok):
                geo[ri, u] = None
                continue
            lo = ok.index(True)
            hi = 4 - ok[::-1].index(True)
            base = start + 4 * u - r + 7
            geo[ri, u] = (base + 3, lo, hi)
    return geo


def _na_block(geo, q0, start_tok, q_ref, qr_ref, k_ref, v_ref, kc_ref, vc_ref, t4_ref, o_ref):
    kc = kc_ref[0]
    vc = vc_ref[0]
    nq = QR * GW
    q = q_ref[0, pl.ds(q0, nq), :]
    lane_q = lax.broadcasted_iota(jnp.int32, (nq, 2 * DH), 1)
    head0 = lane_q < DH
    zero = jnp.zeros_like(q)
    q_h = (jnp.where(head0, q, zero), jnp.where(head0, zero, q))
    sc = _dot_nt(jnp.concatenate(q_h, axis=0), kc)
    half = QR * QC
    head0_t = lax.broadcasted_iota(jnp.int32, (half, 2 * DH), 1) < DH
    zero_t = jnp.zeros((half, 2 * DH), BF16)
    lane_t = lax.broadcasted_iota(jnp.int32, (QC, 2 * DH), 1)
    masks = {}
    for g in geo.values():
        if g is not None and g[1:] != (0, 4) and g[1:] not in masks:
            lo, hi = g[1:]
            masks[lo, hi] = jnp.logical_and(lane_t >= KC * lo, lane_t < KC * hi)
    pzero = jnp.zeros((QC, 2 * DH), BF16)

    def tile_slabs(m):
        return [(GW - QC, QC), (0, QC)] if m == NM - 1 else [(QC * m, KC)]

    def qk(m):
        kt = jnp.concatenate([k_ref[0, pl.ds(start_tok + GW * bi + off, n), :]
                              for bi in range(BAND) for off, n in tile_slabs(m)], axis=0)
        qt = jnp.concatenate([qr_ref[0, pl.ds(q0 + GW * ri + QC * m, QC), :]
                              for ri in range(QR)], axis=0)
        qt = jnp.concatenate([jnp.where(head0_t, qt, zero_t), jnp.where(head0_t, zero_t, qt)],
                             axis=0)
        return _dot_nt(qt, kt)

    def softmax(m, s):
        last = m == NM - 1
        p_rows, pc_rows, dens = [], [], []
        for hh in range(2):
            for ri in range(QR):
                rs = QC * (QR * hh + ri)
                rc = nq * hh + GW * ri + QC * m
                sc_t = sc[rc:rc + QC]
                mvec = jnp.maximum(sc_t[:, :2 * DH], sc_t[:, 2 * DH:])
                tiles = {}
                for u in range(BAND // 4):
                    g = geo[ri, u]
                    if g is None:
                        continue
                    t = s[rs:rs + QC, 2 * DH * u:2 * DH * (u + 1)] + t4_ref[hh, int(last), g[0]]
                    if g[1:] != (0, 4):
                        t = jnp.where(masks[g[1:]], t, NEG)
                    tiles[u] = t
                    mvec = jnp.maximum(mvec, t)
                mx = jnp.max(mvec, axis=-1, keepdims=True)
                pc = jnp.exp(sc_t - mx)
                acc = pc[:, :2 * DH] + pc[:, 2 * DH:]
                prow = []
                for u in range(BAND // 4):
                    if u in tiles:
                        p = jnp.exp(tiles[u] - mx)
                        acc = acc + p
                        prow.append(p.astype(BF16))
                    else:
                        prow.append(pzero)
                p_rows.append(jnp.concatenate(prow, axis=1))
                pc_rows.append(pc.astype(BF16))
                dens.append(jnp.sum(acc, axis=-1, keepdims=True))
        return (jnp.concatenate(p_rows, axis=0), jnp.concatenate(pc_rows, axis=0),
                jnp.concatenate(dens, axis=0))

    def pv(m, p, pc, den):
        vt = jnp.concatenate([v_ref[0, pl.ds(start_tok + GW * bi + off, n), :]
                              for bi in range(BAND) for off, n in tile_slabs(m)], axis=0)
        o = (_dot(p, vt) + _dot(pc, vc)) / den
        ot = jnp.where(head0_t, o[:half], o[half:])
        for ri in range(QR):
            piece = ot[QC * ri:QC * (ri + 1)]
            if m == NM - 1:
                o_ref[0, pl.ds(q0 + GW * ri + GW - QSHIFT, QSHIFT), :] = piece[:QSHIFT]
                o_ref[0, pl.ds(q0 + GW * ri, QSHIFT), :] = piece[QSHIFT:]
            else:
                o_ref[0, pl.ds(q0 + GW * ri + QC * m + QSHIFT, QC), :] = piece

    scores, probs = {}, {}
    for step in NA_ORDER.split():
        m = int(step[1])
        if step[0] == "q":
            scores[m] = qk(m)
        elif step[0] == "s":
            probs[m] = softmax(m, scores.pop(m))
        else:
            pv(m, *probs.pop(m))


def _na_kernel(q_ref, qr_ref, k_ref, v_ref, kc_ref, vc_ref, t4_ref, o_ref):
    nq = QR * GW
    nj = L // nq
    rows = L // GW
    args = (q_ref, qr_ref, k_ref, v_ref, kc_ref, vc_ref, t4_ref, o_ref)
    _na_block(_na_geometry(0, 0), 0, 0, *args)

    def interior(j, carry):
        q0 = pl.multiple_of(j * nq, nq)
        start_tok = pl.multiple_of(j * nq - 4 * GW, 4 * GW)
        _na_block(_na_geometry(QR, QR - 4), q0, start_tok, *args)
        return carry

    lax.fori_loop(1, nj - 1, interior, 0)
    _na_block(_na_geometry(rows - QR, rows - BAND), L - nq, (rows - BAND) * GW, *args)


def _na_call(q, qr, kr, v, kc, vc, t4, nb):
    full = lambda n: pl.BlockSpec((1, n, 2 * DH), lambda b, hp: (b, 0, hp))
    return pl.pallas_call(
        _na_kernel,
        grid=(nb, NH // 2),
        in_specs=[full(L), full(L), full(L), full(L), full(CTX), full(CTX),
                  pl.BlockSpec((2, 2, NT4, QC, 2 * DH), lambda b, hp: (hp, 0, 0, 0, 0))],
        out_specs=full(L),
        out_shape=jax.ShapeDtypeStruct((nb, L, NAW), F32),
        compiler_params=_cp(("arbitrary", "arbitrary")),
        name="na",
    )(q, qr, kr, v, kc, vc, t4)


_R2 = math.sqrt(0.5)


def _bfly(e, o, k, n, sign):
    er, ei = e
    orr, oi = o
    if k == 0:
        tr, ti = orr, oi
    elif 4 * k == n:
        if sign < 0:
            return (er + oi, ei - orr), (er - oi, ei + orr)
        return (er - oi, ei + orr), (er + oi, ei - orr)
    elif 8 * k == n:
        if sign < 0:
            tr, ti = (orr + oi) * _R2, (oi - orr) * _R2
        else:
            tr, ti = (orr - oi) * _R2, (oi + orr) * _R2
    elif 8 * k == 3 * n:
        if sign < 0:
            tr, ti = (oi - orr) * _R2, (orr + oi) * (-_R2)
        else:
            tr, ti = (orr + oi) * (-_R2), (orr - oi) * _R2
    else:
        w = cmath.exp(sign * 2j * math.pi * k / n)
        tr = orr * w.real - oi * w.imag
        ti = orr * w.imag + oi * w.real
    return (er + tr, ei + ti), (er - tr, ei - ti)


def _fft(xs, sign):
    n = len(xs)
    if n == 1:
        return xs
    ev = _fft(xs[0::2], sign)
    od = _fft(xs[1::2], sign)
    out = [None] * n
    for k in range(n // 2):
        out[k], out[k + n // 2] = _bfly(ev[k], od[k], k, n, sign)
    return out


def _for_row_tiles(row0, nrows, unrolled, body):
    if unrolled:
        for r in range(row0, row0 + nrows, 8):
            body(pl.ds(r, 8))
    else:
        def step(i, carry):
            body(pl.ds(pl.multiple_of(row0 + i * 8, 8), 8))
            return carry
        lax.fori_loop(0, nrows // 8, step, 0)


def _coarse_fwd(x_ref, a_ref, twr_ref, twi_ref, row0, nrows, nblocks, unrolled=False):
    def body(rows):
        for c in range(NB // 128):
            xs = []
            for n1 in range(NC):
                lo = n1 * NB + c * 128
                x = x_ref[rows, lo:lo + 128]
                w = cmath.exp(-2j * math.pi * n1 / (4 * NC))
                if nblocks == 2 * NC:
                    lo2 = (n1 + NC) * NB + c * 128
                    x2 = x_ref[rows, lo2:lo2 + 128]
                    w2 = cmath.exp(-2j * math.pi * (n1 + NC) / (4 * NC))
                    xs.append((x * w.real + x2 * w2.real, x * w.imag + x2 * w2.imag))
                elif n1 == 0:
                    xs.append((x, jnp.zeros_like(x)))
                else:
                    xs.append((x * w.real, x * w.imag))
            ys = _fft(xs, -1)
            for m in range(NC):
                cs = slice(c * 128, (c + 1) * 128)
                tr = twr_ref[m, :, cs]
                ti = twi_ref[m, :, cs]
                ar, ai = ys[m]
                a_ref[m, rows, c * 128:(c + 1) * 128] = ar * tr - ai * ti
                a_ref[m, rows, NB + c * 128:NB + (c + 1) * 128] = ar * ti + ai * tr

    _for_row_tiles(row0, nrows, unrolled, body)


def _coarse_inv(b_ref, twr_ref, twi_ref, z_ref, gate_ref, bias_ref, o_ref, row0, nrows,
                unrolled=False):
    def body(rows):
        bias = bias_ref[rows, :]
        for c in range(NB // 128):
            cs = slice(c * 128, (c + 1) * 128)
            bs = []
            for m in range(NC):
                br = b_ref[m, rows, c * 128:(c + 1) * 128]
                bi = b_ref[m, rows, NB + c * 128:NB + (c + 1) * 128]
                tr = twr_ref[m, :, cs]
                ti = twi_ref[m, :, cs]
                bs.append((br * tr + bi * ti, bi * tr - br * ti))
            ys = _fft(bs, +1)
            for n1 in range(NC):
                w = cmath.exp(2j * math.pi * n1 / (4 * NC))
                yr, yi = ys[n1]
                y = yr if n1 == 0 else yr * w.real - yi * w.imag
                lo = n1 * NB + c * 128
                z = z_ref[rows, lo:lo + 128]
                o_ref[rows, lo:lo + 128] = gate_ref[rows, lo:lo + 128] * (y + bias * z)

    _for_row_tiles(row0, nrows, unrolled, body)


def _hymlp_kernel(zt_ref, w1_ref, b1_ref, w2_ref, b2_ref, fr_ref, o_ref):
    fr = fr_ref[...]
    h1 = jnp.sin(fr * (_dot3(w1_ref[...], zt_ref[...]) + b1_ref[...]))
    o_ref[...] = jnp.sin(fr * (_dot3(w2_ref[...], h1) + b2_ref[...]))


def _hymlp_call(zt, w1t, b1, w2t, b2, fr):
    full = lambda s: pl.BlockSpec(s, lambda i: (0,) * len(s))
    return pl.pallas_call(
        _hymlp_kernel,
        grid=(1,),
        in_specs=[full(zt.shape), full(w1t.shape), full(b1.shape),
                  full(w2t.shape), full(b2.shape), full(fr.shape)],
        out_specs=full((HY_FFN, NFFT)),
        out_shape=jax.ShapeDtypeStruct((HY_FFN, NFFT), F32),
        compiler_params=_cp(("arbitrary",)),
        name="hymlp",
    )(zt, w1t, b1, w2t, b2, fr)


def _filt_kernel(h_ref, wf_ref, wb_ref, delta_ref, tpos_ref, sgn_ref, twr_ref, twi_ref,
                 fc_ref, kf_ref, kt_ref, a_ref):
    ht = h_ref[...]
    dec = jnp.exp(-delta_ref[...] * tpos_ref[...]) * sgn_ref[...]
    kt_ref[:, :L] = _dot3(wf_ref[...], ht[:, :L]) * dec[:, :L]
    kt_ref[:, L:] = _dot3(wb_ref[...], ht[:, L:]) * dec[:, L:]
    _coarse_fwd(kt_ref, a_ref, twr_ref, twi_ref, 0, CB, 2 * NC)
    a2 = a_ref[...].reshape(NC * CB, 2 * NB)
    kf_ref[0] = (_dot3(a2, fc_ref[...]) * (2.0 / NFFT)).reshape(NC, CB, 2 * NB)


def _filt_call(hidt, w3f, w3b, delta, tpos, sgn, twr, twi, fc):
    nblk = 2 * HYW // CB
    per = HYW // CB
    return pl.pallas_call(
        _filt_kernel,
        grid=(nblk,),
        in_specs=[_const_spec((HY_FFN, NFFT)),
                  pl.BlockSpec((CB, HY_FFN), lambda i: (i, 0)),
                  pl.BlockSpec((CB, HY_FFN), lambda i: (i, 0)),
                  pl.BlockSpec((CB, 1), lambda i: (i, 0)),
                  _const_spec((1, NFFT)),
                  _const_spec((1, NFFT)),
                  _const_spec((NC, 8, NB)),
                  _const_spec((NC, 8, NB)),
                  _const_spec((2 * NB, 2 * NB))],
        out_specs=pl.BlockSpec((1, NC, CB, 2 * NB), lambda i: (i // per, 0, i % per, 0)),
        out_shape=jax.ShapeDtypeStruct((2, NC, HYW, 2 * NB), F32),
        scratch_shapes=[pltpu.VMEM((CB, NFFT), F32),
                        pltpu.VMEM((NC, CB, 2 * NB), F32)],
        compiler_params=_cp(("arbitrary",)),
        name="filt",
    )(hidt, w3f, w3b, delta, tpos, sgn, twr, twi, fc)


def _short_conv_t(u_ref, w_ref, b_ref, pad_ref, o_ref):
    pad_ref[0:8, :] = jnp.zeros((8, CB), F32)
    pad_ref[L + 8:L + 16, :] = jnp.zeros((8, CB), F32)
    pad_ref[8:L + 8, :] = u_ref[0]
    w = w_ref[...]
    ii = lax.broadcasted_iota(jnp.int32, (CB, CB), 0)
    jj = lax.broadcasted_iota(jnp.int32, (CB, CB), 1)
    eye = jnp.where(ii == jj, 1.0, 0.0).astype(BF16)
    for ch in range(L // TM):
        r = 8 + TM * ch
        s = (pad_ref[r - 1:r - 1 + TM, :] * w[0:1] + pad_ref[r:r + TM, :] * w[1:2]
             + pad_ref[r + 1:r + 1 + TM, :] * w[2:3] + b_ref[...])
        hi, lo = _split(s)
        o_ref[:, TM * ch:TM * (ch + 1)] = _dot_nt(eye, hi) + _dot_nt(eye, lo)


def _spectral_mul(x_ref, kf_ref, order, y_ref, rows):
    for m in range(NC):
        xr = x_ref[m, rows, :NB]
        xi = x_ref[m, rows, NB:]
        kr = kf_ref[order, m, rows, :NB]
        ki = kf_ref[order, m, rows, NB:]
        y_ref[m, rows, :NB] = xr * kr - xi * ki
        y_ref[m, rows, NB:] = xr * ki + xi * kr


HYENA_ORDER = (
    ("fwd", 0, 0), ("dft", 0, 0), ("fwd", 0, 1), ("dft", 0, 1),
    ("mul", 0, 0), ("idft", 0, 0), ("mul", 0, 1), ("idft", 0, 1),
    ("inv", 0, 0), ("fwd", 1, 0), ("dft", 1, 0), ("inv", 0, 1), ("fwd", 1, 1), ("dft", 1, 1),
    ("mul", 1, 0), ("idft", 1, 0), ("mul", 1, 1), ("idft", 1, 1),
    ("inv", 1, 0), ("inv", 1, 1),
)


def _hyena_kernel(v_ref, x1_ref, x2_ref, cwv_ref, cw1_ref, cw2_ref, cbv_ref, cb1_ref, cb2_ref,
                  kf_ref, bias_ref, twr_ref, twi_ref, fc_ref, fi_ref, o_ref,
                  z_ref, g1_ref, g2_ref, a_ref, x_ref, pad_ref):
    _short_conv_t(v_ref, cwv_ref, cbv_ref, pad_ref, z_ref)
    _short_conv_t(x1_ref, cw1_ref, cb1_ref, pad_ref, g1_ref)
    _short_conv_t(x2_ref, cw2_ref, cb2_ref, pad_ref, g2_ref)
    hb = CB // 2

    def matmul(rows, w_ref):
        a2 = a_ref[:, rows, :].reshape(NC * hb, 2 * NB).astype(BF16)
        x_ref[:, rows, :] = _dot(a2, w_ref[...]).reshape(NC, hb, 2 * NB)

    def stage(name, order, half):
        row0 = half * hb
        rows = slice(row0, row0 + hb)
        gate_ref, dst_ref = ((g1_ref, z_ref), (g2_ref, o_ref.at[0]))[order]
        if name == "fwd":
            _coarse_fwd(z_ref, a_ref, twr_ref, twi_ref, row0, hb, NC, unrolled=True)
        elif name == "dft":
            matmul(rows, fc_ref)
        elif name == "mul":
            _spectral_mul(x_ref, kf_ref, order, a_ref, rows)
        elif name == "idft":
            matmul(rows, fi_ref)
        else:
            _coarse_inv(x_ref, twr_ref, twi_ref, z_ref, gate_ref, bias_ref.at[order], dst_ref,
                        row0, hb, unrolled=True)

    for name, order, half in HYENA_ORDER:
        stage(name, order, half)


def _hyena_call(u, cw, cb, kf, bias, twr, twi, fc, fi, nb):
    per = HYW // CB
    chan = lambda part: pl.BlockSpec((1, L, CB), lambda c, b: (b, 0, part * per + c))
    roww = lambda n, part: pl.BlockSpec((n, CB), lambda c, b: (0, part * per + c))
    return pl.pallas_call(
        _hyena_kernel,
        grid=(per, nb),
        in_specs=[chan(0), chan(1), chan(2),
                  roww(3, 0), roww(3, 1), roww(3, 2), roww(1, 0), roww(1, 1), roww(1, 2),
                  pl.BlockSpec((2, NC, CB, 2 * NB), lambda c, b: (0, 0, c, 0),
                               pipeline_mode=pl.Buffered(1)),
                  pl.BlockSpec((2, CB, 128), lambda c, b: (0, c, 0)),
                  _const_spec((NC, 8, NB)),
                  _const_spec((NC, 8, NB)),
                  _const_spec((2 * NB, 2 * NB)),
                  _const_spec((2 * NB, 2 * NB))],
        out_specs=pl.BlockSpec((1, CB, L), lambda c, b: (b, c, 0)),
        out_shape=jax.ShapeDtypeStruct((nb, HYW, L), F32),
        scratch_shapes=[pltpu.VMEM((CB, L), F32),
                        pltpu.VMEM((CB, L), F32),
                        pltpu.VMEM((CB, L), F32),
                        pltpu.VMEM((NC, CB, 2 * NB), F32),
                        pltpu.VMEM((NC, CB, 2 * NB), F32),
                        pltpu.VMEM((L + 16, CB), F32)],
        compiler_params=_cp(("arbitrary", "arbitrary")),
        name="hyena",
    )(u, u, u, cw, cw, cw, cb, cb, cb, kf, bias, twr, twi, fc, fi)


FFN_CHUNKS = ((0, 1024), (1024, 2048), (2048, FFN))


def _tail_kernel(x_ref, yna_ref, yhy_ref, gna_ref, ghy_ref, mod_ref, g2_ref, gf_ref,
                 wna_ref, why_ref, wout_ref, w1_ref, w3_ref, w2_ref, o_ref):
    m = mod_ref[0]
    a = _dot(yna_ref[0].astype(BF16), wna_ref[...])
    b = _dot_tn(yhy_ref[0].astype(BF16), why_ref[...])
    mix = (_sigmoid(gna_ref[0].astype(F32)) * a + _sigmoid(ghy_ref[0].astype(F32)) * b)
    x1 = x_ref[0] + m[:, 2 * D:3 * D] * _dot(mix.astype(BF16), wout_ref[...])
    h2 = _rms_mod(x1, g2_ref[...], m[:, 3 * D:4 * D], m[:, 4 * D:5 * D]).astype(BF16)
    y = None
    for lo, hi in FFN_CHUNKS:
        p = _dot(h2, w1_ref[:, lo:hi])
        act = (p * _sigmoid(p) * _dot(h2, w3_ref[:, lo:hi])).astype(BF16)
        part = _dot(act, w2_ref[lo:hi, :])
        y = part if y is None else y + part
    x2 = x1 + m[:, 5 * D:6 * D] * y
    ms = jnp.mean(x2 * x2, axis=-1, keepdims=True)
    o_ref[0] = x2 * lax.rsqrt(ms + EPS) * gf_ref[...]


def _tail_call(x, yna, yhyt, gna, ghy, mod3, g2, gf, wna, why, wout, w1, w3, w2, nb):
    nt = L // TM
    tok = lambda w: pl.BlockSpec((1, TM, w), lambda b, i: (b, i, 0))
    row = lambda: pl.BlockSpec((1, D), lambda b, i: (0, 0))
    return pl.pallas_call(
        _tail_kernel,
        grid=(nb, nt),
        in_specs=[tok(D), tok(NAW),
                  pl.BlockSpec((1, HYW, TM), lambda b, i: (b, 0, i)),
                  tok(D), tok(D),
                  pl.BlockSpec((1, 1, NMOD * D), lambda b, i: (b, 0, 0)),
                  row(), row(),
                  _const_spec((NAW, D)), _const_spec((HYW, D)), _const_spec((D, D)),
                  _const_spec((D, FFN)), _const_spec((D, FFN)), _const_spec((FFN, D))],
        out_specs=tok(D),
        out_shape=jax.ShapeDtypeStruct((nb, L, D), F32),
        compiler_params=_cp(("arbitrary", "arbitrary")),
        name="tail",
    )(x, yna, yhyt, gna, ghy, mod3, g2, gf, wna, why, wout, w1, w3, w2)


def _rope_tables():
    t = jnp.arange(L)
    rows = (t // GW).astype(F32)
    cols = (t % GW).astype(F32)
    nf = DH // 4
    inv = THETA ** (-jnp.arange(nf, dtype=F32) / nf)
    ar = rows[:, None] * inv
    ac = cols[:, None] * inv
    cos = jnp.concatenate([jnp.cos(ar), jnp.cos(ar), jnp.cos(ac), jnp.cos(ac)], axis=-1)
    sin = jnp.concatenate([-jnp.sin(ar), jnp.sin(ar), -jnp.sin(ac), jnp.sin(ac)], axis=-1)
    return jnp.tile(cos, (1, NH)), jnp.tile(sin, (1, NH))


def _bias_tables(rpb):
    span = QC + KC
    f = jnp.pad(rpb.astype(F32), ((0, 0), (0, 0), (8, span - 8 - 31)), constant_values=NEG)
    b = jnp.tile(f, (1, 1, QC))[:, :, :QC * (span - 1)].reshape(NH, 15, QC, span - 1)
    t = b[..., QC - 1:QC - 1 + KC]
    cq = jnp.arange(QC)[:, None]
    kl = jnp.arange(KC)[None, :]
    ok_a = (kl - cq >= 0) & (kl - cq < 16)
    ok_b = jnp.where(cq < QSHIFT, kl < 16, kl >= 16)
    g = jnp.stack([jnp.where(ok_a, t, NEG), jnp.where(ok_b, t, NEG)], axis=1)
    gx = jnp.pad(g, ((0, 0), (0, 0), (3, 3), (0, 0), (0, 0)), constant_values=NEG)
    return jnp.concatenate([gx[:, :, i:i + NT4] for i in range(4)], axis=-1)


def _dft_tables():
    n2 = jnp.arange(NB)
    ang = (2.0 * math.pi / NB) * ((n2[:, None] * n2[None, :]) % NB).astype(F32)
    fr = jnp.cos(ang)
    fi = -jnp.sin(ang)
    fwd = jnp.concatenate([jnp.concatenate([fr, fi], 1), jnp.concatenate([-fi, fr], 1)], 0)
    inv = jnp.concatenate([jnp.concatenate([fr, -fi], 1), jnp.concatenate([fi, fr], 1)], 0)
    m = jnp.arange(NC)
    tang = (2.0 * math.pi / (2 * NFFT)) * ((n2[None, :] * (4 * m[:, None] + 1)) % (2 * NFFT)).astype(F32)
    twr = jnp.broadcast_to(jnp.cos(tang)[:, None, :], (NC, 8, NB))
    twi = jnp.broadcast_to(-jnp.sin(tang)[:, None, :], (NC, 8, NB))
    return fwd, inv, twr, twi


def _filter_inputs():
    t = jnp.linspace(0.0, 1.0, L, dtype=F32)[:, None]
    w = 2.0 * math.pi * jnp.arange(L, dtype=F32)[:, None] / L
    bands = jnp.linspace(1e-4, HY_BANDS - 1, HY_BANDS, dtype=F32)
    z = jnp.concatenate([t, jnp.cos(bands * w), jnp.sin(-bands * w)], axis=-1)
    zb = jnp.concatenate([z[0:1], z[:0:-1]], axis=0)
    zt = jnp.concatenate([z, zb], axis=0).T
    zt = jnp.pad(zt, ((0, HY_FFN - HY_EMB), (0, 0)))
    tl = t[:, 0]
    tpos = jnp.concatenate([tl, tl[0:1], tl[:0:-1]])[None, :]
    sgn = jnp.concatenate([jnp.ones((L,), F32), jnp.zeros((1,), F32), -jnp.ones((L - 1,), F32)])[None, :]
    min_decay = math.log(1e-2) / 1.5
    max_decay = math.log(1e-2) / 0.3
    deltas = jnp.abs(jnp.linspace(min_decay, max_decay, HYW, dtype=F32))
    return zt, tpos, sgn, deltas


def kernel(x, c, ctx, c_ctx, w_ada, b_ada, norm1_g, norm2_g, w_in, na_rpb, hy_conv_w, hy_conv_b,
           hy_ffn_w1, hy_ffn_b1, hy_ffn_w2, hy_ffn_b2, hy_sin_freq, hy_ffn_w3, hy_bias,
           w_na_o, w_hy_o, w_out, ffn_w1, ffn_w3, ffn_w2, final_g):
    nb = x.shape[0]
    assert x.shape[1:] == (L, D) and w_ada.shape[0] == 1 and nb < 16

    cvec = jnp.zeros((16, D), F32).at[:nb].set(c).at[nb].set(c_ctx)
    mod = _mod_call(cvec, w_ada[0], b_ada[0][None, :])
    mod3 = mod[:, None, :]

    wi = w_in[0]
    g1 = norm1_g[0][None, :]
    kc, vc = _ctx_call(ctx, mod3, g1, wi[:, NAW:3 * NAW].astype(BF16), nb)

    cos, sin = _rope_tables()
    q, qr, kr, v, gna, ghy, u = _inproj_call(x, mod3, g1, wi.astype(BF16), cos, sin, nb)

    yna = _na_call(q, qr, kr, v, kc, vc, _bias_tables(na_rpb[0]), nb)

    fwd, inv, twr, twi = _dft_tables()
    zt, tpos, sgn, deltas = _filter_inputs()
    col = lambda a: a.astype(F32)[:, None]
    w1t = jnp.pad(hy_ffn_w1[0].T, ((0, 0), (0, HY_FFN - HY_EMB)))
    hidt = _hymlp_call(zt, w1t, col(hy_ffn_b1[0]), hy_ffn_w2[0].T, col(hy_ffn_b2[0]),
                       col(hy_sin_freq[0]))
    w3t = hy_ffn_w3[0].T
    kf = _filt_call(hidt, w3t[:2 * HYW], w3t[2 * HYW:], jnp.tile(deltas, 2)[:, None],
                    tpos, sgn, twr, twi, fwd)

    bias = jnp.broadcast_to(hy_bias[0][:, :, None], (2, HYW, 128))
    yhyt = _hyena_call(u, hy_conv_w[0], hy_conv_b[0][None, :], kf, bias, twr, twi,
                       fwd.astype(BF16), inv.astype(BF16), nb)

    return _tail_call(x, yna, yhyt, gna, ghy, mod3, norm2_g[0][None, :], final_g[None, :],
                      w_na_o[0].astype(BF16), w_hy_o[0].astype(BF16), w_out[0].astype(BF16),
                      ffn_w1[0].astype(BF16), ffn_w3[0].astype(BF16), ffn_w2[0].astype(BF16), nb)
```

```python
import cmath
import functools
import math

import jax
import jax.numpy as jnp
from jax import lax
from jax.experimental import pallas as pl
from jax.experimental.pallas import tpu as pltpu

F32 = jnp.float32
BF16 = jnp.bfloat16

D = 1024
L = 4096
GW = 64
CTX = 256
DH = 64
NH = 8
NAW = NH * DH
HYW = 512
FFN = 2816
NMOD = 6
EPS = 1e-6
NEG = -1e30
THETA = 10000.0
HY_EMB = 33
HY_BANDS = 16
HY_FFN = 64

NFFT = 2 * L
NC = 16
NB = 256
TM = 512
CB = 128
QR = 8
BAND = 16
QC = 16
KC = 32
NM = GW // QC
QSHIFT = 8
NT4 = 18
NA_ORDER = "q0 s0 q1 s1 v0 q2 s2 v1 q3 s3 v2 v3"
VMEM_LIMIT = 56 * 1024 * 1024


def _cp(sem):
    return pltpu.CompilerParams(dimension_semantics=sem, vmem_limit_bytes=VMEM_LIMIT)


def _const_spec(shape):
    nd = len(shape)
    return pl.BlockSpec(shape, lambda *_: (0,) * nd, pipeline_mode=pl.Buffered(1))


def _dot(a, b):
    return jnp.dot(a, b, preferred_element_type=F32)


def _dot_nt(a, b):
    return lax.dot_general(a, b, (((1,), (1,)), ((), ())), preferred_element_type=F32)


def _dot_tn(a, b):
    return lax.dot_general(a, b, (((0,), (0,)), ((), ())), preferred_element_type=F32)


def _split(a):
    hi = a.astype(BF16)
    lo = (a - hi.astype(F32)).astype(BF16)
    return hi, lo


def _dot3(a, b, dot=_dot):
    ah, al = _split(a)
    bh, bl = _split(b)
    return dot(ah, bh) + dot(al, bh) + dot(ah, bl)


def _sigmoid(x):
    return 1.0 / (1.0 + jnp.exp(-x))


def _rms_mod(x, g, shift, scale):
    ms = jnp.mean(x * x, axis=-1, keepdims=True)
    y = x * lax.rsqrt(ms + EPS) * g
    return y * (1.0 + scale) + shift


def _mod_kernel(c_ref, w_ref, b_ref, o_ref):
    c = c_ref[...]
    s = c * _sigmoid(c)
    o_ref[...] = _dot3(s, w_ref[...]) + b_ref[...]


def _mod_call(cvec, w_ada, b_ada):
    nt = 4
    tn = NMOD * D // nt
    return pl.pallas_call(
        _mod_kernel,
        grid=(nt,),
        in_specs=[pl.BlockSpec((16, D), lambda i: (0, 0)),
                  pl.BlockSpec((D, tn), lambda i: (0, i)),
                  pl.BlockSpec((1, tn), lambda i: (0, i))],
        out_specs=pl.BlockSpec((16, tn), lambda i: (0, i)),
        out_shape=jax.ShapeDtypeStruct((16, NMOD * D), F32),
        compiler_params=_cp(("arbitrary",)),
        name="mod",
    )(cvec, w_ada, b_ada)


def _ctx_kernel(x_ref, mod_ref, g_ref, w_ref, k_ref, v_ref):
    m = mod_ref[0]
    h = _rms_mod(x_ref[0], g_ref[...], m[:, 0:D], m[:, D:2 * D])
    kv = _dot(h.astype(BF16), w_ref[...])
    k_ref[0] = kv[:, :NAW].astype(BF16)
    v_ref[0] = kv[:, NAW:].astype(BF16)


def _ctx_call(ctx, mod3, g1, w_kv, nb):
    return pl.pallas_call(
        _ctx_kernel,
        grid=(nb,),
        in_specs=[pl.BlockSpec((1, CTX, D), lambda b: (b, 0, 0)),
                  pl.BlockSpec((1, 1, NMOD * D), lambda b: (nb, 0, 0)),
                  pl.BlockSpec((1, D), lambda b: (0, 0)),
                  _const_spec((D, 2 * NAW))],
        out_specs=[pl.BlockSpec((1, CTX, NAW), lambda b: (b, 0, 0)),
                   pl.BlockSpec((1, CTX, NAW), lambda b: (b, 0, 0))],
        out_shape=[jax.ShapeDtypeStruct((nb, CTX, NAW), BF16)] * 2,
        compiler_params=_cp(("arbitrary",)),
        name="ctx",
    )(ctx, mod3, g1, w_kv)


def _rope(t, cos, sin_signed, first):
    up = pltpu.roll(t, NAW - 16, axis=1)
    down = pltpu.roll(t, 16, axis=1)
    return t * cos + jnp.where(first, up, down) * sin_signed


def _rot_cols(t):
    t3 = t.reshape(TM // GW, GW, t.shape[-1])
    return jnp.concatenate([t3[:, QSHIFT:], t3[:, :QSHIFT]], axis=1).reshape(t.shape)


def _inproj_kernel(x_ref, mod_ref, g_ref, w_ref, cos_ref, sin_ref,
                   q_ref, qr_ref, kr_ref, v_ref, gna_ref, ghy_ref, u_ref):
    m = mod_ref[0]
    hb = _rms_mod(x_ref[0], g_ref[...], m[:, 0:D], m[:, D:2 * D]).astype(BF16)
    cos = jnp.tile(cos_ref[...], (1, NH // 2))
    sin = jnp.tile(sin_ref[...], (1, NH // 2))
    lane = lax.broadcasted_iota(jnp.int32, (TM, NAW), 1)
    first = (lane & 16) == 0
    q = _dot(hb, w_ref[:, 0:NAW]) * 0.125
    q_ref[0] = _rot_cols(q).astype(BF16)
    qr_ref[0] = _rot_cols(_rope(q, cos, sin, first)).astype(BF16)
    k = _dot(hb, w_ref[:, NAW:2 * NAW])
    kr_ref[0] = _rope(k, cos, sin, first).astype(BF16)
    v_ref[0] = _dot(hb, w_ref[:, 2 * NAW:3 * NAW]).astype(BF16)
    u_ref[0] = _dot(hb, w_ref[:, 3 * NAW:3 * NAW + 3 * HYW])
    g0 = 3 * NAW + 3 * HYW
    gna_ref[0] = _dot(hb, w_ref[:, g0:g0 + D]).astype(BF16)
    ghy_ref[0] = _dot(hb, w_ref[:, g0 + D:g0 + 2 * D]).astype(BF16)


def _inproj_call(x, mod3, g1, w, cos, sin, nb):
    nt = L // TM
    tok = lambda w_: pl.BlockSpec((1, TM, w_), lambda i, b: (b, i, 0))
    return pl.pallas_call(
        _inproj_kernel,
        grid=(nt, nb),
        in_specs=[tok(D),
                  pl.BlockSpec((1, 1, NMOD * D), lambda i, b: (b, 0, 0)),
                  pl.BlockSpec((1, D), lambda i, b: (0, 0)),
                  _const_spec((D, 3 * NAW + 3 * HYW + 2 * D)),
                  pl.BlockSpec((TM, 2 * DH), lambda i, b: (i, 0)),
                  pl.BlockSpec((TM, 2 * DH), lambda i, b: (i, 0))],
        out_specs=[tok(NAW), tok(NAW), tok(NAW), tok(NAW), tok(D), tok(D), tok(3 * HYW)],
        out_shape=[jax.ShapeDtypeStruct((nb, L, NAW), BF16)] * 4
        + [jax.ShapeDtypeStruct((nb, L, D), BF16)] * 2
        + [jax.ShapeDtypeStruct((nb, L, 3 * HYW), F32)],
        compiler_params=_cp(("arbitrary", "arbitrary")),
        name="inproj",
    )(x, mod3, g1, w, cos, sin)


def _na_geometry(r0, start):
    rows = L // GW
    geo = {}
    for ri in range(QR):
        r = r0 + ri
        ws = min(max(r - 4, 0), rows - 8)
        for u in range(BAND // 4):
            ok = [ws <= start + 4 * u + i < ws + 8 for i in range(4)]
            if not any(ok):
                geo[ri, u] = None
                continue
            lo = ok.index(True)
            hi = 4 - ok[::-1].index(True)
            base = start + 4 * u - r + 7
            geo[ri, u] = (base + 3, lo, hi)
    return geo


def _na_block(geo, q0, start_tok, q_ref, qr_ref, k_ref, v_ref, kc_ref, vc_ref, t4_ref, o_ref):
    kc = kc_ref[0]
    vc = vc_ref[0]
    nq = QR * GW
    q = q_ref[0, pl.ds(q0, nq), :]
    lane_q = lax.broadcasted_iota(jnp.int32, (nq, 2 * DH), 1)
    head0 = lane_q < DH
    zero = jnp.zeros_like(q)
    q_h = (jnp.where(head0, q, zero), jnp.where(head0, zero, q))
    sc = _dot_nt(jnp.concatenate(q_h, axis=0), kc)
    half = QR * QC
    head0_t = lax.broadcasted_iota(jnp.int32, (half, 2 * DH), 1) < DH
    zero_t = jnp.zeros((half, 2 * DH), BF16)
    lane_t = lax.broadcasted_iota(jnp.int32, (QC, 2 * DH), 1)
    masks = {}
    for g in geo.values():
        if g is not None and g[1:] != (0, 4) and g[1:] not in masks:
            lo, hi = g[1:]
            masks[lo, hi] = jnp.logical_and(lane_t >= KC * lo, lane_t < KC * hi)
    pzero = jnp.zeros((QC, 2 * DH), BF16)

    def tile_slabs(m):
        return [(GW - QC, QC), (0, QC)] if m == NM - 1 else [(QC * m, KC)]

    def qk(m):
        kt = jnp.concatenate([k_ref[0, pl.ds(start_tok + GW * bi + off, n), :]
                              for bi in range(BAND) for off, n in tile_slabs(m)], axis=0)
        qt = jnp.concatenate([qr_ref[0, pl.ds(q0 + GW * ri + QC * m, QC), :]
                              for ri in range(QR)], axis=0)
        qt = jnp.concatenate([jnp.where(head0_t, qt, zero_t), jnp.where(head0_t, zero_t, qt)],
                             axis=0)
        return _dot_nt(qt, kt)

    def softmax(m, s):
        last = m == NM - 1
        p_rows, pc_rows, dens = [], [], []
        for hh in range(2):
            for ri in range(QR):
                rs = QC * (QR * hh + ri)
                rc = nq * hh + GW * ri + QC * m
                sc_t = sc[rc:rc + QC]
                mvec = jnp.maximum(sc_t[:, :2 * DH], sc_t[:, 2 * DH:])
                tiles = {}
                for u in range(BAND // 4):
                    g = geo[ri, u]
                    if g is None:
                        continue
                    t = s[rs:rs + QC, 2 * DH * u:2 * DH * (u + 1)] + t4_ref[hh, int(last), g[0]]
                    if g[1:] != (0, 4):
                        t = jnp.where(masks[g[1:]], t, NEG)
                    tiles[u] = t
                    mvec = jnp.maximum(mvec, t)
                mx = jnp.max(mvec, axis=-1, keepdims=True)
                pc = jnp.exp(sc_t - mx)
                acc = pc[:, :2 * DH] + pc[:, 2 * DH:]
                prow = []
                for u in range(BAND // 4):
                    if u in tiles:
                        p = jnp.exp(tiles[u] - mx)
                        acc = acc + p
                        prow.append(p.astype(BF16))
                    else:
                        prow.append(pzero)
                p_rows.append(jnp.concatenate(prow, axis=1))
                pc_rows.append(pc.astype(BF16))
                dens.append(jnp.sum(acc, axis=-1, keepdims=True))
        return (jnp.concatenate(p_rows, axis=0), jnp.concatenate(pc_rows, axis=0),
                jnp.concatenate(dens, axis=0))

    def pv(m, p, pc, den):
        vt = jnp.concatenate([v_ref[0, pl.ds(start_tok + GW * bi + off, n), :]
                              for bi in range(BAND) for off, n in tile_slabs(m)], axis=0)
        o = _dot(jnp.concatenate([p, pc], axis=1), jnp.concatenate([vt, vc], axis=0)) / den
        ot = jnp.where(head0_t, o[:half], o[half:])
        for ri in range(QR):
            piece = ot[QC * ri:QC * (ri + 1)]
            if m == NM - 1:
                o_ref[0, pl.ds(q0 + GW * ri + GW - QSHIFT, QSHIFT), :] = piece[:QSHIFT]
                o_ref[0, pl.ds(q0 + GW * ri, QSHIFT), :] = piece[QSHIFT:]
            else:
                o_ref[0, pl.ds(q0 + GW * ri + QC * m + QSHIFT, QC), :] = piece

    scores, probs = {}, {}
    for step in NA_ORDER.split():
        m = int(step[1])
        if step[0] == "q":
            scores[m] = qk(m)
        elif step[0] == "s":
            probs[m] = softmax(m, scores.pop(m))
        else:
            pv(m, *probs.pop(m))


def _na_kernel(q_ref, qr_ref, k_ref, v_ref, kc_ref, vc_ref, t4_ref, o_ref):
    nq = QR * GW
    nj = L // nq
    rows = L // GW
    args = (q_ref, qr_ref, k_ref, v_ref, kc_ref, vc_ref, t4_ref, o_ref)
    _na_block(_na_geometry(0, 0), 0, 0, *args)

    def interior(j, carry):
        q0 = pl.multiple_of(j * nq, nq)
        start_tok = pl.multiple_of(j * nq - 4 * GW, 4 * GW)
        _na_block(_na_geometry(QR, QR - 4), q0, start_tok, *args)
        return carry

    lax.fori_loop(1, nj - 1, interior, 0)
    _na_block(_na_geometry(rows - QR, rows - BAND), L - nq, (rows - BAND) * GW, *args)


def _na_call(q, qr, kr, v, kc, vc, t4, nb):
    full = lambda n: pl.BlockSpec((1, n, 2 * DH), lambda b, hp: (b, 0, hp))
    return pl.pallas_call(
        _na_kernel,
        grid=(nb, NH // 2),
        in_specs=[full(L), full(L), full(L), full(L), full(CTX), full(CTX),
                  pl.BlockSpec((2, 2, NT4, QC, 2 * DH), lambda b, hp: (hp, 0, 0, 0, 0))],
        out_specs=full(L),
        out_shape=jax.ShapeDtypeStruct((nb, L, NAW), F32),
        compiler_params=_cp(("arbitrary", "arbitrary")),
        name="na",
    )(q, qr, kr, v, kc, vc, t4)


_R2 = math.sqrt(0.5)


def _bfly(e, o, k, n, sign):
    er, ei = e
    orr, oi = o
    if k == 0:
        tr, ti = orr, oi
    elif 4 * k == n:
        if sign < 0:
            return (er + oi, ei - orr), (er - oi, ei + orr)
        return (er - oi, ei + orr), (er + oi, ei - orr)
    elif 8 * k == n:
        if sign < 0:
            tr, ti = (orr + oi) * _R2, (oi - orr) * _R2
        else:
            tr, ti = (orr - oi) * _R2, (oi + orr) * _R2
    elif 8 * k == 3 * n:
        if sign < 0:
            tr, ti = (oi - orr) * _R2, (orr + oi) * (-_R2)
        else:
            tr, ti = (orr + oi) * (-_R2), (orr - oi) * _R2
    else:
        w = cmath.exp(sign * 2j * math.pi * k / n)
        tr = orr * w.real - oi * w.imag
        ti = orr * w.imag + oi * w.real
    return (er + tr, ei + ti), (er - tr, ei - ti)


def _fft(xs, sign):
    n = len(xs)
    if n == 1:
        return xs
    ev = _fft(xs[0::2], sign)
    od = _fft(xs[1::2], sign)
    out = [None] * n
    for k in range(n // 2):
        out[k], out[k + n // 2] = _bfly(ev[k], od[k], k, n, sign)
    return out


def _for_row_tiles(row0, nrows, unrolled, body):
    if unrolled:
        for r in range(row0, row0 + nrows, 8):
            body(pl.ds(r, 8))
    else:
        def step(i, carry):
            body(pl.ds(pl.multiple_of(row0 + i * 8, 8), 8))
            return carry
        lax.fori_loop(0, nrows // 8, step, 0)


def _coarse_fwd(x_ref, a_ref, twr_ref, twi_ref, row0, nrows, nblocks, unrolled=False):
    def body(rows):
        for c in range(NB // 128):
            xs = []
            for n1 in range(NC):
                lo = n1 * NB + c * 128
                x = x_ref[rows, lo:lo + 128]
                w = cmath.exp(-2j * math.pi * n1 / (4 * NC))
                if nblocks == 2 * NC:
                    lo2 = (n1 + NC) * NB + c * 128
                    x2 = x_ref[rows, lo2:lo2 + 128]
                    w2 = cmath.exp(-2j * math.pi * (n1 + NC) / (4 * NC))
                    xs.append((x * w.real + x2 * w2.real, x * w.imag + x2 * w2.imag))
                elif n1 == 0:
                    xs.append((x, jnp.zeros_like(x)))
                else:
                    xs.append((x * w.real, x * w.imag))
            ys = _fft(xs, -1)
            for m in range(NC):
                cs = slice(c * 128, (c + 1) * 128)
                tr = twr_ref[m, :, cs]
                ti = twi_ref[m, :, cs]
                ar, ai = ys[m]
                a_ref[m, rows, c * 128:(c + 1) * 128] = ar * tr - ai * ti
                a_ref[m, rows, NB + c * 128:NB + (c + 1) * 128] = ar * ti + ai * tr

    _for_row_tiles(row0, nrows, unrolled, body)


def _coarse_inv(b_ref, twr_ref, twi_ref, z_ref, gate_ref, bias_ref, o_ref, row0, nrows,
                unrolled=False):
    def body(rows):
        bias = bias_ref[rows, :]
        for c in range(NB // 128):
            cs = slice(c * 128, (c + 1) * 128)
            bs = []
            for m in range(NC):
                br = b_ref[m, rows, c * 128:(c + 1) * 128]
                bi = b_ref[m, rows, NB + c * 128:NB + (c + 1) * 128]
                tr = twr_ref[m, :, cs]
                ti = twi_ref[m, :, cs]
                bs.append((br * tr + bi * ti, bi * tr - br * ti))
            ys = _fft(bs, +1)
            for n1 in range(NC):
                w = cmath.exp(2j * math.pi * n1 / (4 * NC))
                yr, yi = ys[n1]
                y = yr if n1 == 0 else yr * w.real - yi * w.imag
                lo = n1 * NB + c * 128
                z = z_ref[rows, lo:lo + 128]
                o_ref[rows, lo:lo + 128] = gate_ref[rows, lo:lo + 128] * (y + bias * z)

    _for_row_tiles(row0, nrows, unrolled, body)


def _hymlp_kernel(zt_ref, w1_ref, b1_ref, w2_ref, b2_ref, fr_ref, o_ref):
    fr = fr_ref[...]
    h1 = jnp.sin(fr * (_dot3(w1_ref[...], zt_ref[...]) + b1_ref[...]))
    o_ref[...] = jnp.sin(fr * (_dot3(w2_ref[...], h1) + b2_ref[...]))


def _hymlp_call(zt, w1t, b1, w2t, b2, fr):
    full = lambda s: pl.BlockSpec(s, lambda i: (0,) * len(s))
    return pl.pallas_call(
        _hymlp_kernel,
        grid=(1,),
        in_specs=[full(zt.shape), full(w1t.shape), full(b1.shape),
                  full(w2t.shape), full(b2.shape), full(fr.shape)],
        out_specs=full((HY_FFN, NFFT)),
        out_shape=jax.ShapeDtypeStruct((HY_FFN, NFFT), F32),
        compiler_params=_cp(("arbitrary",)),
        name="hymlp",
    )(zt, w1t, b1, w2t, b2, fr)


def _filt_kernel(h_ref, wf_ref, wb_ref, delta_ref, tpos_ref, sgn_ref, twr_ref, twi_ref,
                 fc_ref, kf_ref, kt_ref, a_ref):
    ht = h_ref[...]
    dec = jnp.exp(-delta_ref[...] * tpos_ref[...]) * sgn_ref[...]
    kt_ref[:, :L] = _dot3(wf_ref[...], ht[:, :L]) * dec[:, :L]
    kt_ref[:, L:] = _dot3(wb_ref[...], ht[:, L:]) * dec[:, L:]
    _coarse_fwd(kt_ref, a_ref, twr_ref, twi_ref, 0, CB, 2 * NC)
    ah, al = _split(a_ref[...].reshape(NC * CB, 2 * NB))
    f = fc_ref[...]
    kf_ref[0] = ((_dot(ah, f) + _dot(al, f)) * (2.0 / NFFT)).reshape(NC, CB, 2 * NB)


def _filt_call(hidt, w3f, w3b, delta, tpos, sgn, twr, twi, fc):
    nblk = 2 * HYW // CB
    per = HYW // CB
    return pl.pallas_call(
        _filt_kernel,
        grid=(nblk,),
        in_specs=[_const_spec((HY_FFN, NFFT)),
                  pl.BlockSpec((CB, HY_FFN), lambda i: (i, 0)),
                  pl.BlockSpec((CB, HY_FFN), lambda i: (i, 0)),
                  pl.BlockSpec((CB, 1), lambda i: (i, 0)),
                  _const_spec((1, NFFT)),
                  _const_spec((1, NFFT)),
                  _const_spec((NC, 8, NB)),
                  _const_spec((NC, 8, NB)),
                  _const_spec((2 * NB, 2 * NB))],
        out_specs=pl.BlockSpec((1, NC, CB, 2 * NB), lambda i: (i // per, 0, i % per, 0)),
        out_shape=jax.ShapeDtypeStruct((2, NC, HYW, 2 * NB), F32),
        scratch_shapes=[pltpu.VMEM((CB, NFFT), F32),
                        pltpu.VMEM((NC, CB, 2 * NB), F32)],
        compiler_params=_cp(("arbitrary",)),
        name="filt",
    )(hidt, w3f, w3b, delta, tpos, sgn, twr, twi, fc)


def _short_conv_t(u_ref, w_ref, b_ref, pad_ref, o_ref):
    pad_ref[0:8, :] = jnp.zeros((8, CB), F32)
    pad_ref[L + 8:L + 16, :] = jnp.zeros((8, CB), F32)
    pad_ref[8:L + 8, :] = u_ref[0]
    w = w_ref[...]
    ii = lax.broadcasted_iota(jnp.int32, (CB, CB), 0)
    jj = lax.broadcasted_iota(jnp.int32, (CB, CB), 1)
    eye = jnp.where(ii == jj, 1.0, 0.0).astype(BF16)
    for ch in range(L // TM):
        r = 8 + TM * ch
        s = (pad_ref[r - 1:r - 1 + TM, :] * w[0:1] + pad_ref[r:r + TM, :] * w[1:2]
             + pad_ref[r + 1:r + 1 + TM, :] * w[2:3] + b_ref[...])
        hi, lo = _split(s)
        o_ref[:, TM * ch:TM * (ch + 1)] = _dot_nt(eye, hi) + _dot_nt(eye, lo)


def _spectral_mul(x_ref, kf_ref, order, y_ref, rows):
    for m in range(NC):
        xr = x_ref[m, rows, :NB]
        xi = x_ref[m, rows, NB:]
        kr = kf_ref[order, m, rows, :NB]
        ki = kf_ref[order, m, rows, NB:]
        y_ref[m, rows, :NB] = xr * kr - xi * ki
        y_ref[m, rows, NB:] = xr * ki + xi * kr


HYENA_ORDER = (
    ("fwd", 0, 0), ("dft", 0, 0), ("fwd", 0, 1), ("dft", 0, 1),
    ("mul", 0, 0), ("idft", 0, 0), ("mul", 0, 1), ("idft", 0, 1),
    ("inv", 0, 0), ("fwd", 1, 0), ("dft", 1, 0), ("inv", 0, 1), ("fwd", 1, 1), ("dft", 1, 1),
    ("mul", 1, 0), ("idft", 1, 0), ("mul", 1, 1), ("idft", 1, 1),
    ("inv", 1, 0), ("inv", 1, 1),
)


def _hyena_kernel(v_ref, x1_ref, x2_ref, cwv_ref, cw1_ref, cw2_ref, cbv_ref, cb1_ref, cb2_ref,
                  kf_ref, bias_ref, twr_ref, twi_ref, fc_ref, fi_ref, o_ref,
                  z_ref, g1_ref, g2_ref, a_ref, x_ref, pad_ref):
    _short_conv_t(v_ref, cwv_ref, cbv_ref, pad_ref, z_ref)
    _short_conv_t(x1_ref, cw1_ref, cb1_ref, pad_ref, g1_ref)
    _short_conv_t(x2_ref, cw2_ref, cb2_ref, pad_ref, g2_ref)
    hb = CB // 2

    def matmul(rows, w_ref):
        a2 = a_ref[:, rows, :].reshape(NC * hb, 2 * NB).astype(BF16)
        x_ref[:, rows, :] = _dot(a2, w_ref[...]).reshape(NC, hb, 2 * NB)

    def stage(name, order, half):
        row0 = half * hb
        rows = slice(row0, row0 + hb)
        gate_ref, dst_ref = ((g1_ref, z_ref), (g2_ref, o_ref.at[0]))[order]
        if name == "fwd":
            _coarse_fwd(z_ref, a_ref, twr_ref, twi_ref, row0, hb, NC, unrolled=True)
        elif name == "dft":
            matmul(rows, fc_ref)
        elif name == "mul":
            _spectral_mul(x_ref, kf_ref, order, a_ref, rows)
        elif name == "idft":
            matmul(rows, fi_ref)
        else:
            _coarse_inv(x_ref, twr_ref, twi_ref, z_ref, gate_ref, bias_ref.at[order], dst_ref,
                        row0, hb, unrolled=True)

    for name, order, half in HYENA_ORDER:
        stage(name, order, half)


def _hyena_call(u, cw, cb, kf, bias, twr, twi, fc, fi, nb):
    per = HYW // CB
    chan = lambda part: pl.BlockSpec((1, L, CB), lambda c, b: (b, 0, part * per + c))
    roww = lambda n, part: pl.BlockSpec((n, CB), lambda c, b: (0, part * per + c))
    return pl.pallas_call(
        _hyena_kernel,
        grid=(per, nb),
        in_specs=[chan(0), chan(1), chan(2),
                  roww(3, 0), roww(3, 1), roww(3, 2), roww(1, 0), roww(1, 1), roww(1, 2),
                  pl.BlockSpec((2, NC, CB, 2 * NB), lambda c, b: (0, 0, c, 0),
                               pipeline_mode=pl.Buffered(1)),
                  pl.BlockSpec((2, CB, 128), lambda c, b: (0, c, 0)),
                  _const_spec((NC, 8, NB)),
                  _const_spec((NC, 8, NB)),
                  _const_spec((2 * NB, 2 * NB)),
                  _const_spec((2 * NB, 2 * NB))],
        out_specs=pl.BlockSpec((1, CB, L), lambda c, b: (b, c, 0)),
        out_shape=jax.ShapeDtypeStruct((nb, HYW, L), F32),
        scratch_shapes=[pltpu.VMEM((CB, L), F32),
                        pltpu.VMEM((CB, L), F32),
                        pltpu.VMEM((CB, L), F32),
                        pltpu.VMEM((NC, CB, 2 * NB), F32),
                        pltpu.VMEM((NC, CB, 2 * NB), F32),
                        pltpu.VMEM((L + 16, CB), F32)],
        compiler_params=_cp(("arbitrary", "arbitrary")),
        name="hyena",
    )(u, u, u, cw, cw, cw, cb, cb, cb, kf, bias, twr, twi, fc, fi)


FFN_CHUNKS = ((0, 1024), (1024, 2048), (2048, FFN))


def _tail_kernel(x_ref, yna_ref, yhy_ref, gna_ref, ghy_ref, mod_ref, g2_ref, gf_ref,
                 wna_ref, why_ref, wout_ref, w1_ref, w3_ref, w2_ref, o_ref):
    m = mod_ref[0]
    a = _dot(yna_ref[0].astype(BF16), wna_ref[...])
    b = _dot_tn(yhy_ref[0].astype(BF16), why_ref[...])
    mix = (_sigmoid(gna_ref[0].astype(F32)) * a + _sigmoid(ghy_ref[0].astype(F32)) * b)
    x1 = x_ref[0] + m[:, 2 * D:3 * D] * _dot(mix.astype(BF16), wout_ref[...])
    h2 = _rms_mod(x1, g2_ref[...], m[:, 3 * D:4 * D], m[:, 4 * D:5 * D]).astype(BF16)
    y = None
    for lo, hi in FFN_CHUNKS:
        p = _dot(h2, w1_ref[:, lo:hi])
        act = (p * _sigmoid(p) * _dot(h2, w3_ref[:, lo:hi])).astype(BF16)
        part = _dot(act, w2_ref[lo:hi, :])
        y = part if y is None else y + part
    x2 = x1 + m[:, 5 * D:6 * D] * y
    ms = jnp.mean(x2 * x2, axis=-1, keepdims=True)
    o_ref[0] = x2 * lax.rsqrt(ms + EPS) * gf_ref[...]


def _tail_call(x, yna, yhyt, gna, ghy, mod3, g2, gf, wna, why, wout, w1, w3, w2, nb):
    nt = L // TM
    tok = lambda w: pl.BlockSpec((1, TM, w), lambda b, i: (b, i, 0))
    row = lambda: pl.BlockSpec((1, D), lambda b, i: (0, 0))
    return pl.pallas_call(
        _tail_kernel,
        grid=(nb, nt),
        in_specs=[tok(D), tok(NAW),
                  pl.BlockSpec((1, HYW, TM), lambda b, i: (b, 0, i)),
                  tok(D), tok(D),
                  pl.BlockSpec((1, 1, NMOD * D), lambda b, i: (b, 0, 0)),
                  row(), row(),
                  _const_spec((NAW, D)), _const_spec((HYW, D)), _const_spec((D, D)),
                  _const_spec((D, FFN)), _const_spec((D, FFN)), _const_spec((FFN, D))],
        out_specs=tok(D),
        out_shape=jax.ShapeDtypeStruct((nb, L, D), F32),
        compiler_params=_cp(("arbitrary", "arbitrary")),
        name="tail",
    )(x, yna, yhyt, gna, ghy, mod3, g2, gf, wna, why, wout, w1, w3, w2)


def _rope_tables():
    t = jnp.arange(L)
    rows = (t // GW).astype(F32)
    cols = (t % GW).astype(F32)
    nf = DH // 4
    inv = THETA ** (-jnp.arange(nf, dtype=F32) / nf)
    ar = rows[:, None] * inv
    ac = cols[:, None] * inv
    cos = jnp.concatenate([jnp.cos(ar), jnp.cos(ar), jnp.cos(ac), jnp.cos(ac)], axis=-1)
    sin = jnp.concatenate([-jnp.sin(ar), jnp.sin(ar), -jnp.sin(ac), jnp.sin(ac)], axis=-1)
    return jnp.tile(cos, (1, 2)), jnp.tile(sin, (1, 2))


def _bias_tables(rpb):
    span = QC + KC
    f = jnp.pad(rpb.astype(F32), ((0, 0), (0, 0), (8, span - 8 - 31)), constant_values=NEG)
    b = jnp.tile(f, (1, 1, QC))[:, :, :QC * (span - 1)].reshape(NH, 15, QC, span - 1)
    t = b[..., QC - 1:QC - 1 + KC]
    cq = jnp.arange(QC)[:, None]
    kl = jnp.arange(KC)[None, :]
    ok_a = (kl - cq >= 0) & (kl - cq < 16)
    ok_b = jnp.where(cq < QSHIFT, kl < 16, kl >= 16)
    g = jnp.stack([jnp.where(ok_a, t, NEG), jnp.where(ok_b, t, NEG)], axis=1)
    gx = jnp.pad(g, ((0, 0), (0, 0), (3, 3), (0, 0), (0, 0)), constant_values=NEG)
    return jnp.concatenate([gx[:, :, i:i + NT4] for i in range(4)], axis=-1)


def _dft_tables():
    n2 = jnp.arange(NB)
    ang = (2.0 * math.pi / NB) * ((n2[:, None] * n2[None, :]) % NB).astype(F32)
    fr = jnp.cos(ang)
    fi = -jnp.sin(ang)
    fwd = jnp.concatenate([jnp.concatenate([fr, fi], 1), jnp.concatenate([-fi, fr], 1)], 0)
    inv = jnp.concatenate([jnp.concatenate([fr, -fi], 1), jnp.concatenate([fi, fr], 1)], 0)
    m = jnp.arange(NC)
    tang = (2.0 * math.pi / (2 * NFFT)) * ((n2[None, :] * (4 * m[:, None] + 1)) % (2 * NFFT)).astype(F32)
    twr = jnp.broadcast_to(jnp.cos(tang)[:, None, :], (NC, 8, NB))
    twi = jnp.broadcast_to(-jnp.sin(tang)[:, None, :], (NC, 8, NB))
    return fwd, inv, twr, twi


def _filter_inputs():
    t = jnp.linspace(0.0, 1.0, L, dtype=F32)[:, None]
    w = 2.0 * math.pi * jnp.arange(L, dtype=F32)[:, None] / L
    bands = jnp.linspace(1e-4, HY_BANDS - 1, HY_BANDS, dtype=F32)
    z = jnp.concatenate([t, jnp.cos(bands * w), jnp.sin(-bands * w)], axis=-1)
    zb = jnp.concatenate([z[0:1], z[:0:-1]], axis=0)
    zt = jnp.concatenate([z, zb], axis=0).T
    zt = jnp.pad(zt, ((0, HY_FFN - HY_EMB), (0, 0)))
    tl = t[:, 0]
    tpos = jnp.concatenate([tl, tl[0:1], tl[:0:-1]])[None, :]
    sgn = jnp.concatenate([jnp.ones((L,), F32), jnp.zeros((1,), F32), -jnp.ones((L - 1,), F32)])[None, :]
    min_decay = math.log(1e-2) / 1.5
    max_decay = math.log(1e-2) / 0.3
    deltas = jnp.abs(jnp.linspace(min_decay, max_decay, HYW, dtype=F32))
    return zt, tpos, sgn, deltas


def kernel(x, c, ctx, c_ctx, w_ada, b_ada, norm1_g, norm2_g, w_in, na_rpb, hy_conv_w, hy_conv_b,
           hy_ffn_w1, hy_ffn_b1, hy_ffn_w2, hy_ffn_b2, hy_sin_freq, hy_ffn_w3, hy_bias,
           w_na_o, w_hy_o, w_out, ffn_w1, ffn_w3, ffn_w2, final_g):
    nb = x.shape[0]
    assert x.shape[1:] == (L, D) and w_ada.shape[0] == 1 and nb < 16

    cvec = jnp.zeros((16, D), F32).at[:nb].set(c).at[nb].set(c_ctx)
    mod = _mod_call(cvec, w_ada[0], b_ada[0][None, :])
    mod3 = mod[:, None, :]

    wi = w_in[0]
    g1 = norm1_g[0][None, :]
    kc, vc = _ctx_call(ctx, mod3, g1, wi[:, NAW:3 * NAW].astype(BF16), nb)

    cos, sin = _rope_tables()
    q, qr, kr, v, gna, ghy, u = _inproj_call(x, mod3, g1, wi.astype(BF16), cos, sin, nb)

    yna = _na_call(q, qr, kr, v, kc, vc, _bias_tables(na_rpb[0]), nb)

    fwd, inv, twr, twi = _dft_tables()
    fwd = fwd.astype(BF16)
    inv = inv.astype(BF16)
    zt, tpos, sgn, deltas = _filter_inputs()
    col = lambda a: a.astype(F32)[:, None]
    w1t = jnp.pad(hy_ffn_w1[0].T, ((0, 0), (0, HY_FFN - HY_EMB)))
    hidt = _hymlp_call(zt, w1t, col(hy_ffn_b1[0]), hy_ffn_w2[0].T, col(hy_ffn_b2[0]),
                       col(hy_sin_freq[0]))
    w3t = hy_ffn_w3[0].T
    kf = _filt_call(hidt, w3t[:2 * HYW], w3t[2 * HYW:], jnp.tile(deltas, 2)[:, None],
                    tpos, sgn, twr, twi, fwd)

    bias = jnp.broadcast_to(hy_bias[0][:, :, None], (2, HYW, 128))
    yhyt = _hyena_call(u, hy_conv_w[0], hy_conv_b[0][None, :], kf, bias, twr, twi, fwd, inv, nb)

    return _tail_call(x, yna, yhyt, gna, ghy, mod3, norm2_g[0][None, :], final_g[None, :],
                      w_na_o[0].astype(BF16), w_hy_o[0].astype(BF16), w_out[0].astype(BF16),
                      ffn_w1[0].astype(BF16), ffn_w3[0].astype(BF16), ffn_w2[0].astype(BF16), nb)
```

```python
import cmath
import functools
import math

import jax
import jax.numpy as jnp
from jax import lax
from jax.experimental import pallas as pl
from jax.experimental.pallas import tpu as pltpu

F32 = jnp.float32
BF16 = jnp.bfloat16

D = 1024
L = 4096
GW = 64
CTX = 256
DH = 64
NH = 8
NAW = NH * DH
HYW = 512
FFN = 2816
NMOD = 6
EPS = 1e-6
NEG = -1e30
THETA = 10000.0
HY_EMB = 33
HY_BANDS = 16
HY_FFN = 64

NFFT = 2 * L
NC = 16
NB = 256
TM = 512
CB = 128
QR = 8
BAND = 16
QC = 16
KC = 32
NM = GW // QC
QSHIFT = 8
NT4 = 18
NA_ORDER = ("c q0", "s0 q1 v0 s1 q2 v1 s2 q3 v2", "s3 v3")
VMEM_LIMIT = 56 * 1024 * 1024


def _cp(sem):
    return pltpu.CompilerParams(dimension_semantics=sem, vmem_limit_bytes=VMEM_LIMIT)


def _const_spec(shape):
    nd = len(shape)
    return pl.BlockSpec(shape, lambda *_: (0,) * nd, pipeline_mode=pl.Buffered(1))


def _dot(a, b):
    return jnp.dot(a, b, preferred_element_type=F32)


def _dot_nt(a, b):
    return lax.dot_general(a, b, (((1,), (1,)), ((), ())), preferred_element_type=F32)


def _dot_tn(a, b):
    return lax.dot_general(a, b, (((0,), (0,)), ((), ())), preferred_element_type=F32)


def _split(a):
    hi = a.astype(BF16)
    lo = (a - hi.astype(F32)).astype(BF16)
    return hi, lo


def _dot3(a, b, dot=_dot):
    ah, al = _split(a)
    bh, bl = _split(b)
    return dot(ah, bh) + dot(al, bh) + dot(ah, bl)


def _sigmoid(x):
    return 1.0 / (1.0 + jnp.exp(-x))


def _rms_mod(x, g, shift, scale):
    ms = jnp.mean(x * x, axis=-1, keepdims=True)
    y = x * lax.rsqrt(ms + EPS) * g
    return y * (1.0 + scale) + shift


def _mod_kernel(c_ref, w_ref, b_ref, o_ref):
    c = c_ref[...]
    s = c * _sigmoid(c)
    o_ref[...] = _dot3(s, w_ref[...]) + b_ref[...]


def _mod_call(cvec, w_ada, b_ada):
    nt = 4
    tn = NMOD * D // nt
    return pl.pallas_call(
        _mod_kernel,
        grid=(nt,),
        in_specs=[pl.BlockSpec((16, D), lambda i: (0, 0)),
                  pl.BlockSpec((D, tn), lambda i: (0, i)),
                  pl.BlockSpec((1, tn), lambda i: (0, i))],
        out_specs=pl.BlockSpec((16, tn), lambda i: (0, i)),
        out_shape=jax.ShapeDtypeStruct((16, NMOD * D), F32),
        compiler_params=_cp(("arbitrary",)),
        name="mod",
    )(cvec, w_ada, b_ada)


def _ctx_kernel(x_ref, mod_ref, g_ref, w_ref, k_ref, v_ref):
    m = mod_ref[0]
    h = _rms_mod(x_ref[...], g_ref[...], m[:, 0:D], m[:, D:2 * D])
    kv = _dot(h.astype(BF16), w_ref[...])
    k_ref[...] = kv[:, :NAW].astype(BF16)
    v_ref[...] = kv[:, NAW:].astype(BF16)


def _ctx_call(ctx, mod3, g1, w_kv, nb):
    group = max(g for g in (1, 2, 4) if nb % g == 0)
    rb = CTX * group
    rows = lambda w: pl.BlockSpec((rb, w), lambda i: (i, 0))
    kc, vc = pl.pallas_call(
        _ctx_kernel,
        grid=(nb // group,),
        in_specs=[rows(D),
                  pl.BlockSpec((1, 1, NMOD * D), lambda i: (nb, 0, 0)),
                  pl.BlockSpec((1, D), lambda i: (0, 0)),
                  _const_spec((D, 2 * NAW))],
        out_specs=[rows(NAW), rows(NAW)],
        out_shape=[jax.ShapeDtypeStruct((nb * CTX, NAW), BF16)] * 2,
        compiler_params=_cp(("arbitrary",)),
        name="ctx",
    )(ctx.reshape(nb * CTX, D), mod3, g1, w_kv)
    return kc.reshape(nb, CTX, NAW), vc.reshape(nb, CTX, NAW)


def _rope(t, cos, sin_signed, first):
    up = pltpu.roll(t, NAW - 16, axis=1)
    down = pltpu.roll(t, 16, axis=1)
    return t * cos + jnp.where(first, up, down) * sin_signed


def _rot_cols(t):
    t3 = t.reshape(TM // GW, GW, t.shape[-1])
    return jnp.concatenate([t3[:, QSHIFT:], t3[:, :QSHIFT]], axis=1).reshape(t.shape)


def _inproj_kernel(x_ref, mod_ref, g_ref, w_ref, cos_ref, sin_ref,
                   q_ref, qr_ref, kr_ref, v_ref, gna_ref, ghy_ref, u_ref):
    m = mod_ref[0]
    hb = _rms_mod(x_ref[0], g_ref[...], m[:, 0:D], m[:, D:2 * D]).astype(BF16)
    cos = jnp.tile(cos_ref[...], (1, NH // 2))
    sin = jnp.tile(sin_ref[...], (1, NH // 2))
    lane = lax.broadcasted_iota(jnp.int32, (TM, NAW), 1)
    first = (lane & 16) == 0
    q = _dot(hb, w_ref[:, 0:NAW]) * 0.125
    q_ref[0] = _rot_cols(q).astype(BF16)
    qr_ref[0] = _rot_cols(_rope(q, cos, sin, first)).astype(BF16)
    k = _dot(hb, w_ref[:, NAW:2 * NAW])
    kr_ref[0] = _rope(k, cos, sin, first).astype(BF16)
    v_ref[0] = _dot(hb, w_ref[:, 2 * NAW:3 * NAW]).astype(BF16)
    u_ref[0] = _dot(hb, w_ref[:, 3 * NAW:3 * NAW + 3 * HYW])
    g0 = 3 * NAW + 3 * HYW
    gna_ref[0] = _dot(hb, w_ref[:, g0:g0 + D]).astype(BF16)
    ghy_ref[0] = _dot(hb, w_ref[:, g0 + D:g0 + 2 * D]).astype(BF16)


def _inproj_call(x, mod3, g1, w, cos, sin, nb):
    nt = L // TM
    tok = lambda w_: pl.BlockSpec((1, TM, w_), lambda i, b: (b, i, 0))
    return pl.pallas_call(
        _inproj_kernel,
        grid=(nt, nb),
        in_specs=[tok(D),
                  pl.BlockSpec((1, 1, NMOD * D), lambda i, b: (b, 0, 0)),
                  pl.BlockSpec((1, D), lambda i, b: (0, 0)),
                  _const_spec((D, 3 * NAW + 3 * HYW + 2 * D)),
                  pl.BlockSpec((TM, 2 * DH), lambda i, b: (i, 0)),
                  pl.BlockSpec((TM, 2 * DH), lambda i, b: (i, 0))],
        out_specs=[tok(NAW), tok(NAW), tok(NAW), tok(NAW), tok(D), tok(D), tok(3 * HYW)],
        out_shape=[jax.ShapeDtypeStruct((nb, L, NAW), BF16)] * 4
        + [jax.ShapeDtypeStruct((nb, L, D), BF16)] * 2
        + [jax.ShapeDtypeStruct((nb, L, 3 * HYW), F32)],
        compiler_params=_cp(("arbitrary", "arbitrary")),
        name="inproj",
    )(x, mod3, g1, w, cos, sin)


def _na_geometry(r0, start):
    rows = L // GW
    geo = {}
    for ri in range(QR):
        r = r0 + ri
        ws = min(max(r - 4, 0), rows - 8)
        for u in range(BAND // 4):
            ok = [ws <= start + 4 * u + i < ws + 8 for i in range(4)]
            if not any(ok):
                geo[ri, u] = None
                continue
            lo = ok.index(True)
            hi = 4 - ok[::-1].index(True)
            base = start + 4 * u - r + 7
            geo[ri, u] = (base + 3, lo, hi)
    return geo


def _na_block(geo, q0, start_tok, q_ref, qr_ref, k_ref, v_ref, kc_ref, vc_ref, t4_ref, o_ref):
    kc = kc_ref[0]
    vc = vc_ref[0]
    nq = QR * GW
    state = {}

    def ctx_scores():
        q = q_ref[0, pl.ds(q0, nq), :]
        head0 = lax.broadcasted_iota(jnp.int32, (nq, 2 * DH), 1) < DH
        zero = jnp.zeros_like(q)
        q_h = (jnp.where(head0, q, zero), jnp.where(head0, zero, q))
        state["sc"] = _dot_nt(jnp.concatenate(q_h, axis=0), kc)

    half = QR * QC
    head0_t = lax.broadcasted_iota(jnp.int32, (half, 2 * DH), 1) < DH
    zero_t = jnp.zeros((half, 2 * DH), BF16)
    lane_t = lax.broadcasted_iota(jnp.int32, (QC, 2 * DH), 1)
    masks = {}
    for g in geo.values():
        if g is not None and g[1:] != (0, 4) and g[1:] not in masks:
            lo, hi = g[1:]
            masks[lo, hi] = jnp.logical_and(lane_t >= KC * lo, lane_t < KC * hi)
    pzero = jnp.zeros((QC, 2 * DH), BF16)

    def tile_slabs(m):
        return [(GW - QC, QC), (0, QC)] if m == NM - 1 else [(QC * m, KC)]

    def qk(m):
        kt = jnp.concatenate([k_ref[0, pl.ds(start_tok + GW * bi + off, n), :]
                              for bi in range(BAND) for off, n in tile_slabs(m)], axis=0)
        qt = jnp.concatenate([qr_ref[0, pl.ds(q0 + GW * ri + QC * m, QC), :]
                              for ri in range(QR)], axis=0)
        qt = jnp.concatenate([jnp.where(head0_t, qt, zero_t), jnp.where(head0_t, zero_t, qt)],
                             axis=0)
        return _dot_nt(qt, kt)

    def softmax(m, s):
        last = m == NM - 1
        p_rows, pc_rows, dens = [], [], []
        for hh in range(2):
            for ri in range(QR):
                rs = QC * (QR * hh + ri)
                rc = nq * hh + GW * ri + QC * m
                sc_t = state["sc"][rc:rc + QC]
                mvec = jnp.maximum(sc_t[:, :2 * DH], sc_t[:, 2 * DH:])
                tiles = {}
                for u in range(BAND // 4):
                    g = geo[ri, u]
                    if g is None:
                        continue
                    t = s[rs:rs + QC, 2 * DH * u:2 * DH * (u + 1)] + t4_ref[hh, int(last), g[0]]
                    if g[1:] != (0, 4):
                        t = jnp.where(masks[g[1:]], t, NEG)
                    tiles[u] = t
                    mvec = jnp.maximum(mvec, t)
                mx = jnp.max(mvec, axis=-1, keepdims=True)
                pc = jnp.exp(sc_t - mx)
                acc = pc[:, :2 * DH] + pc[:, 2 * DH:]
                prow = []
                for u in range(BAND // 4):
                    if u in tiles:
                        p = jnp.exp(tiles[u] - mx)
                        acc = acc + p
                        prow.append(p.astype(BF16))
                    else:
                        prow.append(pzero)
                p_rows.append(jnp.concatenate(prow, axis=1))
                pc_rows.append(pc.astype(BF16))
                dens.append(jnp.sum(acc, axis=-1, keepdims=True))
        return (jnp.concatenate(p_rows, axis=0), jnp.concatenate(pc_rows, axis=0),
                jnp.concatenate(dens, axis=0))

    def pv(m, p, pc, den):
        vt = jnp.concatenate([v_ref[0, pl.ds(start_tok + GW * bi + off, n), :]
                              for bi in range(BAND) for off, n in tile_slabs(m)], axis=0)
        o = _dot(jnp.concatenate([p, pc], axis=1), jnp.concatenate([vt, vc], axis=0)) / den
        ot = jnp.where(head0_t, o[:half], o[half:])
        for ri in range(QR):
            piece = ot[QC * ri:QC * (ri + 1)]
            if m == NM - 1:
                o_ref[0, pl.ds(q0 + GW * ri + GW - QSHIFT, QSHIFT), :] = piece[:QSHIFT]
                o_ref[0, pl.ds(q0 + GW * ri, QSHIFT), :] = piece[QSHIFT:]
            else:
                o_ref[0, pl.ds(q0 + GW * ri + QC * m + QSHIFT, QC), :] = piece

    def run(stage):
        if stage == "c":
            ctx_scores()
            return
        m = int(stage[1])
        if stage[0] == "q":
            state["s", m] = qk(m)
        elif stage[0] == "s":
            state["p", m] = softmax(m, state.pop(("s", m)))
        else:
            pv(m, *state.pop(("p", m)))

    return run


def _na_kernel(q_ref, qr_ref, k_ref, v_ref, kc_ref, vc_ref, t4_ref, o_ref):
    nq = QR * GW
    nj = L // nq
    rows = L // GW
    args = (q_ref, qr_ref, k_ref, v_ref, kc_ref, vc_ref, t4_ref, o_ref)
    blocks = []
    for j in range(nj):
        r0 = QR * j
        start = min(max(r0 - 4, 0), rows - BAND)
        blocks.append(_na_block(_na_geometry(r0, start), nq * j, GW * start, *args))
    head, body, tail = NA_ORDER
    for stage in head.split():
        blocks[0](stage)
    for j in range(nj):
        for stage in body.split():
            blocks[j](stage)
        if j + 1 < nj:
            for stage in head.split():
                blocks[j + 1](stage)
        for stage in tail.split():
            blocks[j](stage)


def _na_call(q, qr, kr, v, kc, vc, t4, nb):
    full = lambda n: pl.BlockSpec((1, n, 2 * DH), lambda b, hp: (b, 0, hp))
    return pl.pallas_call(
        _na_kernel,
        grid=(nb, NH // 2),
        in_specs=[full(L), full(L), full(L), full(L), full(CTX), full(CTX),
                  pl.BlockSpec((2, 2, NT4, QC, 2 * DH), lambda b, hp: (hp, 0, 0, 0, 0))],
        out_specs=full(L),
        out_shape=jax.ShapeDtypeStruct((nb, L, NAW), F32),
        compiler_params=_cp(("arbitrary", "arbitrary")),
        name="na",
    )(q, qr, kr, v, kc, vc, t4)


_R2 = math.sqrt(0.5)


def _bfly(e, o, k, n, sign):
    er, ei = e
    orr, oi = o
    if k == 0:
        tr, ti = orr, oi
    elif 4 * k == n:
        if sign < 0:
            return (er + oi, ei - orr), (er - oi, ei + orr)
        return (er - oi, ei + orr), (er + oi, ei - orr)
    elif 8 * k == n:
        if sign < 0:
            tr, ti = (orr + oi) * _R2, (oi - orr) * _R2
        else:
            tr, ti = (orr - oi) * _R2, (oi + orr) * _R2
    elif 8 * k == 3 * n:
        if sign < 0:
            tr, ti = (oi - orr) * _R2, (orr + oi) * (-_R2)
        else:
            tr, ti = (orr + oi) * (-_R2), (orr - oi) * _R2
    else:
        w = cmath.exp(sign * 2j * math.pi * k / n)
        tr = orr * w.real - oi * w.imag
        ti = orr * w.imag + oi * w.real
    return (er + tr, ei + ti), (er - tr, ei - ti)


def _fft(xs, sign):
    n = len(xs)
    if n == 1:
        return xs
    ev = _fft(xs[0::2], sign)
    od = _fft(xs[1::2], sign)
    out = [None] * n
    for k in range(n // 2):
        out[k], out[k + n // 2] = _bfly(ev[k], od[k], k, n, sign)
    return out


def _for_row_tiles(row0, nrows, unrolled, body):
    if unrolled:
        for r in range(row0, row0 + nrows, 8):
            body(pl.ds(r, 8))
    else:
        def step(i, carry):
            body(pl.ds(pl.multiple_of(row0 + i * 8, 8), 8))
            return carry
        lax.fori_loop(0, nrows // 8, step, 0)


def _coarse_fwd(x_ref, a_ref, twr_ref, twi_ref, row0, nrows, nblocks, unrolled=False):
    def body(rows):
        for c in range(NB // 128):
            xs = []
            for n1 in range(NC):
                lo = n1 * NB + c * 128
                x = x_ref[rows, lo:lo + 128]
                w = cmath.exp(-2j * math.pi * n1 / (4 * NC))
                if nblocks == 2 * NC:
                    lo2 = (n1 + NC) * NB + c * 128
                    x2 = x_ref[rows, lo2:lo2 + 128]
                    w2 = cmath.exp(-2j * math.pi * (n1 + NC) / (4 * NC))
                    xs.append((x * w.real + x2 * w2.real, x * w.imag + x2 * w2.imag))
                elif n1 == 0:
                    xs.append((x, jnp.zeros_like(x)))
                else:
                    xs.append((x * w.real, x * w.imag))
            ys = _fft(xs, -1)
            for m in range(NC):
                cs = slice(c * 128, (c + 1) * 128)
                tr = twr_ref[m, :, cs]
                ti = twi_ref[m, :, cs]
                ar, ai = ys[m]
                a_ref[m, rows, c * 128:(c + 1) * 128] = ar * tr - ai * ti
                a_ref[m, rows, NB + c * 128:NB + (c + 1) * 128] = ar * ti + ai * tr

    _for_row_tiles(row0, nrows, unrolled, body)


def _coarse_inv(b_ref, twr_ref, twi_ref, z_ref, gate_ref, bias_ref, o_ref, row0, nrows,
                unrolled=False):
    def body(rows):
        bias = bias_ref[rows, :]
        for c in range(NB // 128):
            cs = slice(c * 128, (c + 1) * 128)
            bs = []
            for m in range(NC):
                br = b_ref[m, rows, c * 128:(c + 1) * 128]
                bi = b_ref[m, rows, NB + c * 128:NB + (c + 1) * 128]
                tr = twr_ref[m, :, cs]
                ti = twi_ref[m, :, cs]
                bs.append((br * tr + bi * ti, bi * tr - br * ti))
            ys = _fft(bs, +1)
            for n1 in range(NC):
                w = cmath.exp(2j * math.pi * n1 / (4 * NC))
                yr, yi = ys[n1]
                y = yr if n1 == 0 else yr * w.real - yi * w.imag
                lo = n1 * NB + c * 128
                z = z_ref[rows, lo:lo + 128]
                o_ref[rows, lo:lo + 128] = gate_ref[rows, lo:lo + 128] * (y + bias * z)

    _for_row_tiles(row0, nrows, unrolled, body)


def _hymlp_kernel(zt_ref, w1_ref, b1_ref, w2_ref, b2_ref, fr_ref, o_ref):
    fr = fr_ref[...]
    h1 = jnp.sin(fr * (_dot3(w1_ref[...], zt_ref[...]) + b1_ref[...]))
    o_ref[...] = jnp.sin(fr * (_dot3(w2_ref[...], h1) + b2_ref[...]))


def _hymlp_call(zt, w1t, b1, w2t, b2, fr):
    full = lambda s: pl.BlockSpec(s, lambda i: (0,) * len(s))
    return pl.pallas_call(
        _hymlp_kernel,
        grid=(1,),
        in_specs=[full(zt.shape), full(w1t.shape), full(b1.shape),
                  full(w2t.shape), full(b2.shape), full(fr.shape)],
        out_specs=full((HY_FFN, NFFT)),
        out_shape=jax.ShapeDtypeStruct((HY_FFN, NFFT), F32),
        compiler_params=_cp(("arbitrary",)),
        name="hymlp",
    )(zt, w1t, b1, w2t, b2, fr)


def _filt_kernel(h_ref, wf_ref, wb_ref, delta_ref, tpos_ref, sgn_ref, twr_ref, twi_ref,
                 fc_ref, kf_ref, kt_ref, a_ref):
    ht = h_ref[...]
    dec = jnp.exp(-delta_ref[...] * tpos_ref[...]) * sgn_ref[...]
    kt_ref[:, :L] = _dot3(wf_ref[...], ht[:, :L]) * dec[:, :L]
    kt_ref[:, L:] = _dot3(wb_ref[...], ht[:, L:]) * dec[:, L:]
    _coarse_fwd(kt_ref, a_ref, twr_ref, twi_ref, 0, CB, 2 * NC)
    ah, al = _split(a_ref[...].reshape(NC * CB, 2 * NB))
    f = fc_ref[...]
    kf_ref[0] = ((_dot(ah, f) + _dot(al, f)) * (2.0 / NFFT)).reshape(NC, CB, 2 * NB)


def _filt_call(hidt, w3f, w3b, delta, tpos, sgn, twr, twi, fc):
    nblk = 2 * HYW // CB
    per = HYW // CB
    return pl.pallas_call(
        _filt_kernel,
        grid=(nblk,),
        in_specs=[_const_spec((HY_FFN, NFFT)),
                  pl.BlockSpec((CB, HY_FFN), lambda i: (i, 0)),
                  pl.BlockSpec((CB, HY_FFN), lambda i: (i, 0)),
                  pl.BlockSpec((CB, 1), lambda i: (i, 0)),
                  _const_spec((1, NFFT)),
                  _const_spec((1, NFFT)),
                  _const_spec((NC, 8, NB)),
                  _const_spec((NC, 8, NB)),
                  _const_spec((2 * NB, 2 * NB))],
        out_specs=pl.BlockSpec((1, NC, CB, 2 * NB), lambda i: (i // per, 0, i % per, 0)),
        out_shape=jax.ShapeDtypeStruct((2, NC, HYW, 2 * NB), F32),
        scratch_shapes=[pltpu.VMEM((CB, NFFT), F32),
                        pltpu.VMEM((NC, CB, 2 * NB), F32)],
        compiler_params=_cp(("arbitrary",)),
        name="filt",
    )(hidt, w3f, w3b, delta, tpos, sgn, twr, twi, fc)


def _short_conv_t(u_ref, w_ref, b_ref, pad_ref, o_ref, two_pass):
    pad_ref[0:8, :] = jnp.zeros((8, CB), F32)
    pad_ref[L + 8:L + 16, :] = jnp.zeros((8, CB), F32)
    pad_ref[8:L + 8, :] = u_ref[0]
    w = w_ref[...]
    ii = lax.broadcasted_iota(jnp.int32, (CB, CB), 0)
    jj = lax.broadcasted_iota(jnp.int32, (CB, CB), 1)
    eye = jnp.where(ii == jj, 1.0, 0.0).astype(BF16)
    for ch in range(L // TM):
        r = 8 + TM * ch
        s = (pad_ref[r - 1:r - 1 + TM, :] * w[0:1] + pad_ref[r:r + TM, :] * w[1:2]
             + pad_ref[r + 1:r + 1 + TM, :] * w[2:3] + b_ref[...])
        if two_pass:
            hi, lo = _split(s)
            o_ref[:, TM * ch:TM * (ch + 1)] = _dot_nt(eye, hi) + _dot_nt(eye, lo)
        else:
            o_ref[:, TM * ch:TM * (ch + 1)] = _dot_nt(eye, s.astype(BF16))


def _spectral_mul(x_ref, kf_ref, order, y_ref, rows):
    for m in range(NC):
        xr = x_ref[m, rows, :NB]
        xi = x_ref[m, rows, NB:]
        kr = kf_ref[order, m, rows, :NB]
        ki = kf_ref[order, m, rows, NB:]
        y_ref[m, rows, :NB] = xr * kr - xi * ki
        y_ref[m, rows, NB:] = xr * ki + xi * kr


HYENA_ORDER = (
    ("fwd", 0, 0), ("dft", 0, 0), ("fwd", 0, 1), ("dft", 0, 1),
    ("mul", 0, 0), ("idft", 0, 0), ("mul", 0, 1), ("idft", 0, 1),
    ("inv", 0, 0), ("fwd", 1, 0), ("dft", 1, 0), ("inv", 0, 1), ("fwd", 1, 1), ("dft", 1, 1),
    ("mul", 1, 0), ("idft", 1, 0), ("mul", 1, 1), ("idft", 1, 1),
    ("inv", 1, 0), ("inv", 1, 1),
)


def _hyena_kernel(v_ref, x1_ref, x2_ref, cwv_ref, cw1_ref, cw2_ref, cbv_ref, cb1_ref, cb2_ref,
                  kf_ref, bias_ref, twr_ref, twi_ref, fc_ref, fi_ref, o_ref,
                  z_ref, g1_ref, g2_ref, a_ref, x_ref, pad_ref):
    _short_conv_t(v_ref, cwv_ref, cbv_ref, pad_ref, z_ref, True)
    _short_conv_t(x1_ref, cw1_ref, cb1_ref, pad_ref, g1_ref, False)
    _short_conv_t(x2_ref, cw2_ref, cb2_ref, pad_ref, g2_ref, False)
    hb = CB // 2

    def matmul(rows, w_ref):
        a2 = a_ref[:, rows, :].reshape(NC * hb, 2 * NB).astype(BF16)
        x_ref[:, rows, :] = _dot(a2, w_ref[...]).reshape(NC, hb, 2 * NB)

    def stage(name, order, half):
        row0 = half * hb
        rows = slice(row0, row0 + hb)
        gate_ref, dst_ref = ((g1_ref, z_ref), (g2_ref, o_ref.at[0]))[order]
        if name == "fwd":
            _coarse_fwd(z_ref, a_ref, twr_ref, twi_ref, row0, hb, NC, unrolled=True)
        elif name == "dft":
            matmul(rows, fc_ref)
        elif name == "mul":
            _spectral_mul(x_ref, kf_ref, order, a_ref, rows)
        elif name == "idft":
            matmul(rows, fi_ref)
        else:
            _coarse_inv(x_ref, twr_ref, twi_ref, z_ref, gate_ref, bias_ref.at[order], dst_ref,
                        row0, hb, unrolled=True)

    for name, order, half in HYENA_ORDER:
        stage(name, order, half)


def _hyena_call(u, cw, cb, kf, bias, twr, twi, fc, fi, nb):
    per = HYW // CB
    chan = lambda part: pl.BlockSpec((1, L, CB), lambda c, b: (b, 0, part * per + c))
    roww = lambda n, part: pl.BlockSpec((n, CB), lambda c, b: (0, part * per + c))
    return pl.pallas_call(
        _hyena_kernel,
        grid=(per, nb),
        in_specs=[chan(0), chan(1), chan(2),
                  roww(3, 0), roww(3, 1), roww(3, 2), roww(1, 0), roww(1, 1), roww(1, 2),
                  pl.BlockSpec((2, NC, CB, 2 * NB), lambda c, b: (0, 0, c, 0),
                               pipeline_mode=pl.Buffered(1)),
                  pl.BlockSpec((2, CB, 128), lambda c, b: (0, c, 0)),
                  _const_spec((NC, 8, NB)),
                  _const_spec((NC, 8, NB)),
                  _const_spec((2 * NB, 2 * NB)),
                  _const_spec((2 * NB, 2 * NB))],
        out_specs=pl.BlockSpec((1, CB, L), lambda c, b: (b, c, 0)),
        out_shape=jax.ShapeDtypeStruct((nb, HYW, L), F32),
        scratch_shapes=[pltpu.VMEM((CB, L), F32),
                        pltpu.VMEM((CB, L), F32),
                        pltpu.VMEM((CB, L), F32),
                        pltpu.VMEM((NC, CB, 2 * NB), F32),
                        pltpu.VMEM((NC, CB, 2 * NB), F32),
                        pltpu.VMEM((L + 16, CB), F32)],
        compiler_params=_cp(("arbitrary", "arbitrary")),
        name="hyena",
    )(u, u, u, cw, cw, cw, cb, cb, cb, kf, bias, twr, twi, fc, fi)


FFN_CHUNKS = ((0, 1024), (1024, 2048), (2048, FFN))


def _tail_kernel(x_ref, yna_ref, yhy_ref, gna_ref, ghy_ref, mod_ref, g2_ref, gf_ref,
                 wna_ref, why_ref, wout_ref, w1_ref, w3_ref, w2_ref, o_ref):
    m = mod_ref[0]
    a = _dot(yna_ref[0].astype(BF16), wna_ref[...])
    b = _dot_tn(yhy_ref[0].astype(BF16), why_ref[...])
    mix = (_sigmoid(gna_ref[0].astype(F32)) * a + _sigmoid(ghy_ref[0].astype(F32)) * b)
    x1 = x_ref[0] + m[:, 2 * D:3 * D] * _dot(mix.astype(BF16), wout_ref[...])
    h2 = _rms_mod(x1, g2_ref[...], m[:, 3 * D:4 * D], m[:, 4 * D:5 * D]).astype(BF16)
    y = None
    for lo, hi in FFN_CHUNKS:
        p = _dot(h2, w1_ref[:, lo:hi])
        act = (p * _sigmoid(p) * _dot(h2, w3_ref[:, lo:hi])).astype(BF16)
        part = _dot(act, w2_ref[lo:hi, :])
        y = part if y is None else y + part
    x2 = x1 + m[:, 5 * D:6 * D] * y
    ms = jnp.mean(x2 * x2, axis=-1, keepdims=True)
    o_ref[0] = x2 * lax.rsqrt(ms + EPS) * gf_ref[...]


def _tail_call(x, yna, yhyt, gna, ghy, mod3, g2, gf, wna, why, wout, w1, w3, w2, nb):
    nt = L // TM
    tok = lambda w: pl.BlockSpec((1, TM, w), lambda b, i: (b, i, 0))
    row = lambda: pl.BlockSpec((1, D), lambda b, i: (0, 0))
    return pl.pallas_call(
        _tail_kernel,
        grid=(nb, nt),
        in_specs=[tok(D), tok(NAW),
                  pl.BlockSpec((1, HYW, TM), lambda b, i: (b, 0, i)),
                  tok(D), tok(D),
                  pl.BlockSpec((1, 1, NMOD * D), lambda b, i: (b, 0, 0)),
                  row(), row(),
                  _const_spec((NAW, D)), _const_spec((HYW, D)), _const_spec((D, D)),
                  _const_spec((D, FFN)), _const_spec((D, FFN)), _const_spec((FFN, D))],
        out_specs=tok(D),
        out_shape=jax.ShapeDtypeStruct((nb, L, D), F32),
        compiler_params=_cp(("arbitrary", "arbitrary")),
        name="tail",
    )(x, yna, yhyt, gna, ghy, mod3, g2, gf, wna, why, wout, w1, w3, w2)


def _rope_tables():
    t = jnp.arange(L)
    rows = (t // GW).astype(F32)
    cols = (t % GW).astype(F32)
    nf = DH // 4
    inv = THETA ** (-jnp.arange(nf, dtype=F32) / nf)
    ar = rows[:, None] * inv
    ac = cols[:, None] * inv
    cos = jnp.concatenate([jnp.cos(ar), jnp.cos(ar), jnp.cos(ac), jnp.cos(ac)], axis=-1)
    sin = jnp.concatenate([-jnp.sin(ar), jnp.sin(ar), -jnp.sin(ac), jnp.sin(ac)], axis=-1)
    return jnp.tile(cos, (1, 2)), jnp.tile(sin, (1, 2))


def _bias_tables(rpb):
    span = QC + KC
    f = jnp.pad(rpb.astype(F32), ((0, 0), (0, 0), (8, span - 8 - 31)), constant_values=NEG)
    b = jnp.tile(f, (1, 1, QC))[:, :, :QC * (span - 1)].reshape(NH, 15, QC, span - 1)
    t = b[..., QC - 1:QC - 1 + KC]
    cq = jnp.arange(QC)[:, None]
    kl = jnp.arange(KC)[None, :]
    ok_a = (kl - cq >= 0) & (kl - cq < 16)
    ok_b = jnp.where(cq < QSHIFT, kl < 16, kl >= 16)
    g = jnp.stack([jnp.where(ok_a, t, NEG), jnp.where(ok_b, t, NEG)], axis=1)
    gx = jnp.pad(g, ((0, 0), (0, 0), (3, 3), (0, 0), (0, 0)), constant_values=NEG)
    return jnp.concatenate([gx[:, :, i:i + NT4] for i in range(4)], axis=-1)


def _dft_tables():
    n2 = jnp.arange(NB)
    ang = (2.0 * math.pi / NB) * ((n2[:, None] * n2[None, :]) % NB).astype(F32)
    fr = jnp.cos(ang)
    fi = -jnp.sin(ang)
    fwd = jnp.concatenate([jnp.concatenate([fr, fi], 1), jnp.concatenate([-fi, fr], 1)], 0)
    inv = jnp.concatenate([jnp.concatenate([fr, -fi], 1), jnp.concatenate([fi, fr], 1)], 0)
    m = jnp.arange(NC)
    tang = (2.0 * math.pi / (2 * NFFT)) * ((n2[None, :] * (4 * m[:, None] + 1)) % (2 * NFFT)).astype(F32)
    twr = jnp.broadcast_to(jnp.cos(tang)[:, None, :], (NC, 8, NB))
    twi = jnp.broadcast_to(-jnp.sin(tang)[:, None, :], (NC, 8, NB))
    return fwd, inv, twr, twi


def _filter_inputs():
    t = jnp.linspace(0.0, 1.0, L, dtype=F32)[:, None]
    w = 2.0 * math.pi * jnp.arange(L, dtype=F32)[:, None] / L
    bands = jnp.linspace(1e-4, HY_BANDS - 1, HY_BANDS, dtype=F32)
    z = jnp.concatenate([t, jnp.cos(bands * w), jnp.sin(-bands * w)], axis=-1)
    zb = jnp.concatenate([z[0:1], z[:0:-1]], axis=0)
    zt = jnp.concatenate([z, zb], axis=0).T
    zt = jnp.pad(zt, ((0, HY_FFN - HY_EMB), (0, 0)))
    tl = t[:, 0]
    tpos = jnp.concatenate([tl, tl[0:1], tl[:0:-1]])[None, :]
    sgn = jnp.concatenate([jnp.ones((L,), F32), jnp.zeros((1,), F32), -jnp.ones((L - 1,), F32)])[None, :]
    min_decay = math.log(1e-2) / 1.5
    max_decay = math.log(1e-2) / 0.3
    deltas = jnp.abs(jnp.linspace(min_decay, max_decay, HYW, dtype=F32))
    return zt, tpos, sgn, deltas


def kernel(x, c, ctx, c_ctx, w_ada, b_ada, norm1_g, norm2_g, w_in, na_rpb, hy_conv_w, hy_conv_b,
           hy_ffn_w1, hy_ffn_b1, hy_ffn_w2, hy_ffn_b2, hy_sin_freq, hy_ffn_w3, hy_bias,
           w_na_o, w_hy_o, w_out, ffn_w1, ffn_w3, ffn_w2, final_g):
    nb = x.shape[0]
    assert x.shape[1:] == (L, D) and w_ada.shape[0] == 1 and nb < 16

    cvec = jnp.zeros((16, D), F32).at[:nb].set(c).at[nb].set(c_ctx)
    mod = _mod_call(cvec, w_ada[0], b_ada[0][None, :])
    mod3 = mod[:, None, :]

    wi = w_in[0]
    g1 = norm1_g[0][None, :]
    kc, vc = _ctx_call(ctx, mod3, g1, wi[:, NAW:3 * NAW].astype(BF16), nb)

    cos, sin = _rope_tables()
    q, qr, kr, v, gna, ghy, u = _inproj_call(x, mod3, g1, wi.astype(BF16), cos, sin, nb)

    yna = _na_call(q, qr, kr, v, kc, vc, _bias_tables(na_rpb[0]), nb)

    fwd, inv, twr, twi = _dft_tables()
    fwd = fwd.astype(BF16)
    inv = inv.astype(BF16)
    zt, tpos, sgn, deltas = _filter_inputs()
    col = lambda a: a.astype(F32)[:, None]
    w1t = jnp.pad(hy_ffn_w1[0].T, ((0, 0), (0, HY_FFN - HY_EMB)))
    hidt = _hymlp_call(zt, w1t, col(hy_ffn_b1[0]), hy_ffn_w2[0].T, col(hy_ffn_b2[0]),
                       col(hy_sin_freq[0]))
    w3t = hy_ffn_w3[0].T
    kf = _filt_call(hidt, w3t[:2 * HYW], w3t[2 * HYW:], jnp.tile(deltas, 2)[:, None],
                    tpos, sgn, twr, twi, fwd)

    bias = jnp.broadcast_to(hy_bias[0][:, :, None], (2, HYW, 128))
    yhyt = _hyena_call(u, hy_conv_w[0], hy_conv_b[0][None, :], kf, bias, twr, twi, fwd, inv, nb)

    return _tail_call(x, yna, yhyt, gna, ghy, mod3, norm2_g[0][None, :], final_g[None, :],
                      w_na_o[0].astype(BF16), w_hy_o[0].astype(BF16), w_out[0].astype(BF16),
                      ffn_w1[0].astype(BF16), ffn_w3[0].astype(BF16), ffn_w2[0].astype(BF16), nb)
```

```python
import cmath
import functools
import math

import jax
import jax.numpy as jnp
from jax import lax
from jax.experimental import pallas as pl
from jax.experimental.pallas import tpu as pltpu

F32 = jnp.float32
BF16 = jnp.bfloat16

D = 1024
L = 4096
GW = 64
CTX = 256
DH = 64
NH = 8
NAW = NH * DH
HYW = 512
FFN = 2816
NMOD = 6
EPS = 1e-6
NEG = -1e30
THETA = 10000.0
HY_EMB = 33
HY_BANDS = 16
HY_FFN = 64

NFFT = 2 * L
NC = 16
NB = 256
TM = 512
CB = 128
QR = 8
BAND = 16
QC = 16
KC = 32
NM = GW // QC
QSHIFT = 8
NT4 = 18
NA_ORDER = ("c q0", "s0 q1 v0", "s1 q2 v1 s2 q3 v2 s3 v3")
VMEM_LIMIT = 56 * 1024 * 1024


def _cp(sem):
    return pltpu.CompilerParams(dimension_semantics=sem, vmem_limit_bytes=VMEM_LIMIT)


def _const_spec(shape):
    nd = len(shape)
    return pl.BlockSpec(shape, lambda *_: (0,) * nd, pipeline_mode=pl.Buffered(1))


def _dot(a, b):
    return jnp.dot(a, b, preferred_element_type=F32)


def _dot_nt(a, b):
    return lax.dot_general(a, b, (((1,), (1,)), ((), ())), preferred_element_type=F32)


def _dot_tn(a, b):
    return lax.dot_general(a, b, (((0,), (0,)), ((), ())), preferred_element_type=F32)


def _split(a):
    hi = a.astype(BF16)
    lo = (a - hi.astype(F32)).astype(BF16)
    return hi, lo


def _dot3(a, b, dot=_dot):
    ah, al = _split(a)
    bh, bl = _split(b)
    return dot(ah, bh) + dot(al, bh) + dot(ah, bl)


def _sigmoid(x):
    return 1.0 / (1.0 + jnp.exp(-x))


def _rms_mod(x, g, shift, scale):
    ms = jnp.mean(x * x, axis=-1, keepdims=True)
    y = x * lax.rsqrt(ms + EPS) * g
    return y * (1.0 + scale) + shift


def _mod_kernel(c_ref, w_ref, b_ref, o_ref):
    c = c_ref[...]
    s = c * _sigmoid(c)
    o_ref[...] = _dot3(s, w_ref[...]) + b_ref[...]


def _mod_call(cvec, w_ada, b_ada):
    nt = 4
    tn = NMOD * D // nt
    return pl.pallas_call(
        _mod_kernel,
        grid=(nt,),
        in_specs=[pl.BlockSpec((16, D), lambda i: (0, 0)),
                  pl.BlockSpec((D, tn), lambda i: (0, i)),
                  pl.BlockSpec((1, tn), lambda i: (0, i))],
        out_specs=pl.BlockSpec((16, tn), lambda i: (0, i)),
        out_shape=jax.ShapeDtypeStruct((16, NMOD * D), F32),
        compiler_params=_cp(("arbitrary",)),
        name="mod",
    )(cvec, w_ada, b_ada)


def _ctx_kernel(x_ref, mod_ref, g_ref, w_ref, k_ref, v_ref):
    m = mod_ref[0]
    h = _rms_mod(x_ref[...], g_ref[...], m[:, 0:D], m[:, D:2 * D])
    kv = _dot(h.astype(BF16), w_ref[...])
    k_ref[...] = kv[:, :NAW].astype(BF16)
    v_ref[...] = kv[:, NAW:].astype(BF16)


def _ctx_call(ctx, mod3, g1, w_kv, nb):
    group = max(g for g in (1, 2, 4) if nb % g == 0)
    rb = CTX * group
    rows = lambda w: pl.BlockSpec((rb, w), lambda i: (i, 0))
    kc, vc = pl.pallas_call(
        _ctx_kernel,
        grid=(nb // group,),
        in_specs=[rows(D),
                  pl.BlockSpec((1, 1, NMOD * D), lambda i: (nb, 0, 0)),
                  pl.BlockSpec((1, D), lambda i: (0, 0)),
                  _const_spec((D, 2 * NAW))],
        out_specs=[rows(NAW), rows(NAW)],
        out_shape=[jax.ShapeDtypeStruct((nb * CTX, NAW), BF16)] * 2,
        compiler_params=_cp(("arbitrary",)),
        name="ctx",
    )(ctx.reshape(nb * CTX, D), mod3, g1, w_kv)
    return kc.reshape(nb, CTX, NAW), vc.reshape(nb, CTX, NAW)


def _rope(t, cos, sin_signed, first):
    up = pltpu.roll(t, NAW - 16, axis=1)
    down = pltpu.roll(t, 16, axis=1)
    return t * cos + jnp.where(first, up, down) * sin_signed


def _rot_cols(t):
    t3 = t.reshape(TM // GW, GW, t.shape[-1])
    return jnp.concatenate([t3[:, QSHIFT:], t3[:, :QSHIFT]], axis=1).reshape(t.shape)


def _inproj_kernel(x_ref, mod_ref, g_ref, w_ref, cos_ref, sin_ref,
                   q_ref, qr_ref, kr_ref, v_ref, gna_ref, ghy_ref, u_ref):
    m = mod_ref[0]
    hb = _rms_mod(x_ref[0], g_ref[...], m[:, 0:D], m[:, D:2 * D]).astype(BF16)
    cos = jnp.tile(cos_ref[...], (1, NH // 2))
    sin = jnp.tile(sin_ref[...], (1, NH // 2))
    lane = lax.broadcasted_iota(jnp.int32, (TM, NAW), 1)
    first = (lane & 16) == 0
    q = _dot(hb, w_ref[:, 0:NAW]) * 0.125
    q_ref[0] = _rot_cols(q).astype(BF16)
    qr_ref[0] = _rot_cols(_rope(q, cos, sin, first)).astype(BF16)
    k = _dot(hb, w_ref[:, NAW:2 * NAW])
    kr_ref[0] = _rope(k, cos, sin, first).astype(BF16)
    v_ref[0] = _dot(hb, w_ref[:, 2 * NAW:3 * NAW]).astype(BF16)
    u_ref[0] = _dot(hb, w_ref[:, 3 * NAW:3 * NAW + 3 * HYW])
    g0 = 3 * NAW + 3 * HYW
    gna_ref[0] = _dot(hb, w_ref[:, g0:g0 + D]).astype(BF16)
    ghy_ref[0] = _dot(hb, w_ref[:, g0 + D:g0 + 2 * D]).astype(BF16)


def _inproj_call(x, mod3, g1, w, cos, sin, nb):
    nt = L // TM
    tok = lambda w_: pl.BlockSpec((1, TM, w_), lambda i, b: (b, i, 0))
    return pl.pallas_call(
        _inproj_kernel,
        grid=(nt, nb),
        in_specs=[tok(D),
                  pl.BlockSpec((1, 1, NMOD * D), lambda i, b: (b, 0, 0)),
                  pl.BlockSpec((1, D), lambda i, b: (0, 0)),
                  _const_spec((D, 3 * NAW + 3 * HYW + 2 * D)),
                  pl.BlockSpec((TM, 2 * DH), lambda i, b: (i, 0)),
                  pl.BlockSpec((TM, 2 * DH), lambda i, b: (i, 0))],
        out_specs=[tok(NAW), tok(NAW), tok(NAW), tok(NAW), tok(D), tok(D), tok(3 * HYW)],
        out_shape=[jax.ShapeDtypeStruct((nb, L, NAW), BF16)] * 4
        + [jax.ShapeDtypeStruct((nb, L, D), BF16)] * 2
        + [jax.ShapeDtypeStruct((nb, L, 3 * HYW), F32)],
        compiler_params=_cp(("arbitrary", "arbitrary")),
        name="inproj",
    )(x, mod3, g1, w, cos, sin)


def _na_geometry(r0, start):
    rows = L // GW
    geo = {}
    for ri in range(QR):
        r = r0 + ri
        ws = min(max(r - 4, 0), rows - 8)
        for u in range(BAND // 4):
            ok = [ws <= start + 4 * u + i < ws + 8 for i in range(4)]
            if not any(ok):
                geo[ri, u] = None
                continue
            lo = ok.index(True)
            hi = 4 - ok[::-1].index(True)
            base = start + 4 * u - r + 7
            geo[ri, u] = (base + 3, lo, hi)
    return geo


def _na_block(geo, q0, start_tok, q_ref, qr_ref, k_ref, v_ref, kc_ref, vc_ref, t4_ref, o_ref):
    kc = kc_ref[0]
    vc = vc_ref[0]
    nq = QR * GW
    state = {}

    def ctx_scores():
        q = q_ref[0, pl.ds(q0, nq), :]
        head0 = lax.broadcasted_iota(jnp.int32, (nq, 2 * DH), 1) < DH
        zero = jnp.zeros_like(q)
        q_h = (jnp.where(head0, q, zero), jnp.where(head0, zero, q))
        state["sc"] = _dot_nt(jnp.concatenate(q_h, axis=0), kc)

    half = QR * QC
    head0_t = lax.broadcasted_iota(jnp.int32, (half, 2 * DH), 1) < DH
    zero_t = jnp.zeros((half, 2 * DH), BF16)
    lane_t = lax.broadcasted_iota(jnp.int32, (QC, 2 * DH), 1)
    masks = {}
    for g in geo.values():
        if g is not None and g[1:] != (0, 4) and g[1:] not in masks:
            lo, hi = g[1:]
            masks[lo, hi] = jnp.logical_and(lane_t >= KC * lo, lane_t < KC * hi)
    pzero = jnp.zeros((QC, 2 * DH), BF16)

    def tile_slabs(m):
        return [(GW - QC, QC), (0, QC)] if m == NM - 1 else [(QC * m, KC)]

    def qk(m):
        kt = jnp.concatenate([k_ref[0, pl.ds(start_tok + GW * bi + off, n), :]
                              for bi in range(BAND) for off, n in tile_slabs(m)], axis=0)
        qt = jnp.concatenate([qr_ref[0, pl.ds(q0 + GW * ri + QC * m, QC), :]
                              for ri in range(QR)], axis=0)
        qt = jnp.concatenate([jnp.where(head0_t, qt, zero_t), jnp.where(head0_t, zero_t, qt)],
                             axis=0)
        return _dot_nt(qt, kt)

    def softmax(m, s):
        last = m == NM - 1
        p_rows, pc_rows, dens = [], [], []
        for hh in range(2):
            for ri in range(QR):
                rs = QC * (QR * hh + ri)
                rc = nq * hh + GW * ri + QC * m
                sc_t = state["sc"][rc:rc + QC]
                mvec = jnp.maximum(sc_t[:, :2 * DH], sc_t[:, 2 * DH:])
                tiles = {}
                for u in range(BAND // 4):
                    g = geo[ri, u]
                    if g is None:
                        continue
                    t = s[rs:rs + QC, 2 * DH * u:2 * DH * (u + 1)] + t4_ref[hh, int(last), g[0]]
                    if g[1:] != (0, 4):
                        t = jnp.where(masks[g[1:]], t, NEG)
                    tiles[u] = t
                    mvec = jnp.maximum(mvec, t)
                mx = jnp.max(mvec, axis=-1, keepdims=True)
                pc = jnp.exp(sc_t - mx)
                acc = pc[:, :2 * DH] + pc[:, 2 * DH:]
                prow = []
                for u in range(BAND // 4):
                    if u in tiles:
                        p = jnp.exp(tiles[u] - mx)
                        acc = acc + p
                        prow.append(p.astype(BF16))
                    else:
                        prow.append(pzero)
                p_rows.append(jnp.concatenate(prow, axis=1))
                pc_rows.append(pc.astype(BF16))
                dens.append(jnp.sum(acc, axis=-1, keepdims=True))
        return (jnp.concatenate(p_rows, axis=0), jnp.concatenate(pc_rows, axis=0),
                jnp.concatenate(dens, axis=0))

    def pv(m, p, pc, den):
        vt = jnp.concatenate([v_ref[0, pl.ds(start_tok + GW * bi + off, n), :]
                              for bi in range(BAND) for off, n in tile_slabs(m)], axis=0)
        o = _dot(jnp.concatenate([p, pc], axis=1), jnp.concatenate([vt, vc], axis=0)) / den
        ot = jnp.where(head0_t, o[:half], o[half:])
        for ri in range(QR):
            piece = ot[QC * ri:QC * (ri + 1)]
            if m == NM - 1:
                o_ref[0, pl.ds(q0 + GW * ri + GW - QSHIFT, QSHIFT), :] = piece[:QSHIFT]
                o_ref[0, pl.ds(q0 + GW * ri, QSHIFT), :] = piece[QSHIFT:]
            else:
                o_ref[0, pl.ds(q0 + GW * ri + QC * m + QSHIFT, QC), :] = piece

    def run(stage):
        if stage == "c":
            ctx_scores()
            return
        m = int(stage[1])
        if stage[0] == "q":
            state["s", m] = qk(m)
        elif stage[0] == "s":
            state["p", m] = softmax(m, state.pop(("s", m)))
        else:
            pv(m, *state.pop(("p", m)))

    return run


def _na_kernel(q_ref, qr_ref, k_ref, v_ref, kc_ref, vc_ref, t4_ref, o_ref):
    nq = QR * GW
    nj = L // nq
    rows = L // GW
    args = (q_ref, qr_ref, k_ref, v_ref, kc_ref, vc_ref, t4_ref, o_ref)
    blocks = []
    for j in range(nj):
        r0 = QR * j
        start = min(max(r0 - 4, 0), rows - BAND)
        blocks.append(_na_block(_na_geometry(r0, start), nq * j, GW * start, *args))
    head, body, tail = NA_ORDER
    for stage in head.split():
        blocks[0](stage)
    for j in range(nj):
        for stage in body.split():
            blocks[j](stage)
        if j + 1 < nj:
            for stage in head.split():
                blocks[j + 1](stage)
        for stage in tail.split():
            blocks[j](stage)


def _na_call(q, qr, kr, v, kc, vc, t4, nb):
    full = lambda n: pl.BlockSpec((1, n, 2 * DH), lambda b, hp: (b, 0, hp))
    return pl.pallas_call(
        _na_kernel,
        grid=(nb, NH // 2),
        in_specs=[full(L), full(L), full(L), full(L), full(CTX), full(CTX),
                  pl.BlockSpec((2, 2, NT4, QC, 2 * DH), lambda b, hp: (hp, 0, 0, 0, 0))],
        out_specs=full(L),
        out_shape=jax.ShapeDtypeStruct((nb, L, NAW), F32),
        compiler_params=_cp(("arbitrary", "arbitrary")),
        name="na",
    )(q, qr, kr, v, kc, vc, t4)


_R2 = math.sqrt(0.5)


def _bfly(e, o, k, n, sign):
    er, ei = e
    orr, oi = o
    if k == 0:
        tr, ti = orr, oi
    elif 4 * k == n:
        if sign < 0:
            return (er + oi, ei - orr), (er - oi, ei + orr)
        return (er - oi, ei + orr), (er + oi, ei - orr)
    elif 8 * k == n:
        if sign < 0:
            tr, ti = (orr + oi) * _R2, (oi - orr) * _R2
        else:
            tr, ti = (orr - oi) * _R2, (oi + orr) * _R2
    elif 8 * k == 3 * n:
        if sign < 0:
            tr, ti = (oi - orr) * _R2, (orr + oi) * (-_R2)
        else:
            tr, ti = (orr + oi) * (-_R2), (orr - oi) * _R2
    else:
        w = cmath.exp(sign * 2j * math.pi * k / n)
        tr = orr * w.real - oi * w.imag
        ti = orr * w.imag + oi * w.real
    return (er + tr, ei + ti), (er - tr, ei - ti)


def _fft(xs, sign):
    n = len(xs)
    if n == 1:
        return xs
    ev = _fft(xs[0::2], sign)
    od = _fft(xs[1::2], sign)
    out = [None] * n
    for k in range(n // 2):
        out[k], out[k + n // 2] = _bfly(ev[k], od[k], k, n, sign)
    return out


def _for_row_tiles(row0, nrows, unrolled, body):
    if unrolled:
        for r in range(row0, row0 + nrows, 8):
            body(pl.ds(r, 8))
    else:
        def step(i, carry):
            body(pl.ds(pl.multiple_of(row0 + i * 8, 8), 8))
            return carry
        lax.fori_loop(0, nrows // 8, step, 0)


def _coarse_fwd(x_ref, a_ref, twr_ref, twi_ref, row0, nrows, nblocks, unrolled=False):
    def body(rows):
        for c in range(NB // 128):
            xs = []
            for n1 in range(NC):
                lo = n1 * NB + c * 128
                x = x_ref[rows, lo:lo + 128]
                w = cmath.exp(-2j * math.pi * n1 / (4 * NC))
                if nblocks == 2 * NC:
                    lo2 = (n1 + NC) * NB + c * 128
                    x2 = x_ref[rows, lo2:lo2 + 128]
                    w2 = cmath.exp(-2j * math.pi * (n1 + NC) / (4 * NC))
                    xs.append((x * w.real + x2 * w2.real, x * w.imag + x2 * w2.imag))
                elif n1 == 0:
                    xs.append((x, jnp.zeros_like(x)))
                else:
                    xs.append((x * w.real, x * w.imag))
            ys = _fft(xs, -1)
            for m in range(NC):
                cs = slice(c * 128, (c + 1) * 128)
                tr = twr_ref[m, :, cs]
                ti = twi_ref[m, :, cs]
                ar, ai = ys[m]
                a_ref[m, rows, c * 128:(c + 1) * 128] = ar * tr - ai * ti
                a_ref[m, rows, NB + c * 128:NB + (c + 1) * 128] = ar * ti + ai * tr

    _for_row_tiles(row0, nrows, unrolled, body)


def _coarse_inv(b_ref, twr_ref, twi_ref, z_ref, gate_ref, bias_ref, o_ref, row0, nrows,
                unrolled=False):
    def body(rows):
        bias = bias_ref[rows, :]
        for c in range(NB // 128):
            cs = slice(c * 128, (c + 1) * 128)
            bs = []
            for m in range(NC):
                br = b_ref[m, rows, c * 128:(c + 1) * 128]
                bi = b_ref[m, rows, NB + c * 128:NB + (c + 1) * 128]
                tr = twr_ref[m, :, cs]
                ti = twi_ref[m, :, cs]
                bs.append((br * tr + bi * ti, bi * tr - br * ti))
            ys = _fft(bs, +1)
            for n1 in range(NC):
                w = cmath.exp(2j * math.pi * n1 / (4 * NC))
                yr, yi = ys[n1]
                y = yr if n1 == 0 else yr * w.real - yi * w.imag
                lo = n1 * NB + c * 128
                z = z_ref[rows, lo:lo + 128]
                o_ref[rows, lo:lo + 128] = gate_ref[rows, lo:lo + 128] * (y + bias * z)

    _for_row_tiles(row0, nrows, unrolled, body)


def _hymlp_kernel(zt_ref, w1_ref, b1_ref, w2_ref, b2_ref, fr_ref, o_ref):
    fr = fr_ref[...]
    h1 = jnp.sin(fr * (_dot3(w1_ref[...], zt_ref[...]) + b1_ref[...]))
    o_ref[...] = jnp.sin(fr * (_dot3(w2_ref[...], h1) + b2_ref[...]))


def _hymlp_call(zt, w1t, b1, w2t, b2, fr):
    full = lambda s: pl.BlockSpec(s, lambda i: (0,) * len(s))
    return pl.pallas_call(
        _hymlp_kernel,
        grid=(1,),
        in_specs=[full(zt.shape), full(w1t.shape), full(b1.shape),
                  full(w2t.shape), full(b2.shape), full(fr.shape)],
        out_specs=full((HY_FFN, NFFT)),
        out_shape=jax.ShapeDtypeStruct((HY_FFN, NFFT), F32),
        compiler_params=_cp(("arbitrary",)),
        name="hymlp",
    )(zt, w1t, b1, w2t, b2, fr)


def _filt_kernel(h_ref, wf_ref, wb_ref, delta_ref, tpos_ref, sgn_ref, twr_ref, twi_ref,
                 fc_ref, kf_ref, kt_ref, a_ref):
    ht = h_ref[...]
    dec = jnp.exp(-delta_ref[...] * tpos_ref[...]) * sgn_ref[...]
    kt_ref[:, :L] = _dot3(wf_ref[...], ht[:, :L]) * dec[:, :L]
    kt_ref[:, L:] = _dot3(wb_ref[...], ht[:, L:]) * dec[:, L:]
    _coarse_fwd(kt_ref, a_ref, twr_ref, twi_ref, 0, CB, 2 * NC)
    ah, al = _split(a_ref[...].reshape(NC * CB, 2 * NB))
    f = fc_ref[...]
    kf_ref[0] = ((_dot(ah, f) + _dot(al, f)) * (2.0 / NFFT)).reshape(NC, CB, 2 * NB)


def _filt_call(hidt, w3f, w3b, delta, tpos, sgn, twr, twi, fc):
    nblk = 2 * HYW // CB
    per = HYW // CB
    return pl.pallas_call(
        _filt_kernel,
        grid=(nblk,),
        in_specs=[_const_spec((HY_FFN, NFFT)),
                  pl.BlockSpec((CB, HY_FFN), lambda i: (i, 0)),
                  pl.BlockSpec((CB, HY_FFN), lambda i: (i, 0)),
                  pl.BlockSpec((CB, 1), lambda i: (i, 0)),
                  _const_spec((1, NFFT)),
                  _const_spec((1, NFFT)),
                  _const_spec((NC, 8, NB)),
                  _const_spec((NC, 8, NB)),
                  _const_spec((2 * NB, 2 * NB))],
        out_specs=pl.BlockSpec((1, NC, CB, 2 * NB), lambda i: (i // per, 0, i % per, 0)),
        out_shape=jax.ShapeDtypeStruct((2, NC, HYW, 2 * NB), F32),
        scratch_shapes=[pltpu.VMEM((CB, NFFT), F32),
                        pltpu.VMEM((NC, CB, 2 * NB), F32)],
        compiler_params=_cp(("arbitrary",)),
        name="filt",
    )(hidt, w3f, w3b, delta, tpos, sgn, twr, twi, fc)


def _short_conv_t(u_ref, w_ref, b_ref, pad_ref, o_ref, two_pass):
    pad_ref[0:8, :] = jnp.zeros((8, CB), F32)
    pad_ref[L + 8:L + 16, :] = jnp.zeros((8, CB), F32)
    pad_ref[8:L + 8, :] = u_ref[0]
    w = w_ref[...]
    ii = lax.broadcasted_iota(jnp.int32, (CB, CB), 0)
    jj = lax.broadcasted_iota(jnp.int32, (CB, CB), 1)
    eye = jnp.where(ii == jj, 1.0, 0.0).astype(BF16)
    for ch in range(L // TM):
        r = 8 + TM * ch
        s = (pad_ref[r - 1:r - 1 + TM, :] * w[0:1] + pad_ref[r:r + TM, :] * w[1:2]
             + pad_ref[r + 1:r + 1 + TM, :] * w[2:3] + b_ref[...])
        if two_pass:
            hi, lo = _split(s)
            o_ref[:, TM * ch:TM * (ch + 1)] = _dot_nt(eye, hi) + _dot_nt(eye, lo)
        else:
            o_ref[:, TM * ch:TM * (ch + 1)] = _dot_nt(eye, s.astype(BF16))


def _spectral_mul(x_ref, kf_ref, order, y_ref, rows):
    for m in range(NC):
        xr = x_ref[m, rows, :NB]
        xi = x_ref[m, rows, NB:]
        kr = kf_ref[order, m, rows, :NB]
        ki = kf_ref[order, m, rows, NB:]
        y_ref[m, rows, :NB] = xr * kr - xi * ki
        y_ref[m, rows, NB:] = xr * ki + xi * kr


HYENA_ORDER = (
    ("fwd", 0, 0), ("dft", 0, 0), ("fwd", 0, 1), ("dft", 0, 1),
    ("mul", 0, 0), ("idft", 0, 0), ("mul", 0, 1), ("idft", 0, 1),
    ("inv", 0, 0), ("fwd", 1, 0), ("dft", 1, 0), ("inv", 0, 1), ("fwd", 1, 1), ("dft", 1, 1),
    ("mul", 1, 0), ("idft", 1, 0), ("mul", 1, 1), ("idft", 1, 1),
    ("inv", 1, 0), ("inv", 1, 1),
)


def _hyena_kernel(v_ref, x1_ref, x2_ref, cwv_ref, cw1_ref, cw2_ref, cbv_ref, cb1_ref, cb2_ref,
                  kf_ref, bias_ref, twr_ref, twi_ref, fc_ref, fi_ref, o_ref,
                  z_ref, g1_ref, g2_ref, a_ref, x_ref, pad_ref):
    _short_conv_t(v_ref, cwv_ref, cbv_ref, pad_ref, z_ref, True)
    _short_conv_t(x1_ref, cw1_ref, cb1_ref, pad_ref, g1_ref, False)
    _short_conv_t(x2_ref, cw2_ref, cb2_ref, pad_ref, g2_ref, False)
    hb = CB // 2

    def matmul(rows, w_ref):
        a2 = a_ref[:, rows, :].reshape(NC * hb, 2 * NB).astype(BF16)
        x_ref[:, rows, :] = _dot(a2, w_ref[...]).reshape(NC, hb, 2 * NB)

    def stage(name, order, half):
        row0 = half * hb
        rows = slice(row0, row0 + hb)
        if name == "fwd":
            _coarse_fwd(z_ref, a_ref, twr_ref, twi_ref, row0, hb, NC, unrolled=True)
        elif name == "dft":
            matmul(rows, fc_ref)
        elif name == "mul":
            _spectral_mul(x_ref, kf_ref, order, a_ref, rows)
        elif name == "idft":
            matmul(rows, fi_ref)
        else:
            gate_ref, dst_ref = ((g1_ref, z_ref), (g2_ref, o_ref.at[0]))[order]
            _coarse_inv(x_ref, twr_ref, twi_ref, z_ref, gate_ref, bias_ref.at[order], dst_ref,
                        row0, hb, unrolled=True)

    for name, order, half in HYENA_ORDER:
        stage(name, order, half)


def _hyena_call(u, cw, cb, kf, bias, twr, twi, fc, fi, nb):
    per = HYW // CB
    chan = lambda part: pl.BlockSpec((1, L, CB), lambda c, b: (b, 0, part * per + c))
    roww = lambda n, part: pl.BlockSpec((n, CB), lambda c, b: (0, part * per + c))
    return pl.pallas_call(
        _hyena_kernel,
        grid=(per, nb),
        in_specs=[chan(0), chan(1), chan(2),
                  roww(3, 0), roww(3, 1), roww(3, 2), roww(1, 0), roww(1, 1), roww(1, 2),
                  pl.BlockSpec((2, NC, CB, 2 * NB), lambda c, b: (0, 0, c, 0),
                               pipeline_mode=pl.Buffered(1)),
                  pl.BlockSpec((2, CB, 128), lambda c, b: (0, c, 0)),
                  _const_spec((NC, 8, NB)),
                  _const_spec((NC, 8, NB)),
                  _const_spec((2 * NB, 2 * NB)),
                  _const_spec((2 * NB, 2 * NB))],
        out_specs=pl.BlockSpec((1, CB, L), lambda c, b: (b, c, 0)),
        out_shape=jax.ShapeDtypeStruct((nb, HYW, L), F32),
        scratch_shapes=[pltpu.VMEM((CB, L), F32),
                        pltpu.VMEM((CB, L), F32),
                        pltpu.VMEM((CB, L), F32),
                        pltpu.VMEM((NC, CB, 2 * NB), F32),
                        pltpu.VMEM((NC, CB, 2 * NB), F32),
                        pltpu.VMEM((L + 16, CB), F32)],
        compiler_params=_cp(("arbitrary", "arbitrary")),
        name="hyena",
    )(u, u, u, cw, cw, cw, cb, cb, cb, kf, bias, twr, twi, fc, fi)


FFN_CHUNKS = ((0, 1024), (1024, 2048), (2048, FFN))


def _tail_kernel(x_ref, yna_ref, yhy_ref, gna_ref, ghy_ref, mod_ref, g2_ref, gf_ref,
                 wna_ref, why_ref, wout_ref, w1_ref, w3_ref, w2_ref, o_ref):
    m = mod_ref[0]
    a = _dot(yna_ref[0].astype(BF16), wna_ref[...])
    b = _dot_tn(yhy_ref[0].astype(BF16), why_ref[...])
    mix = (_sigmoid(gna_ref[0].astype(F32)) * a + _sigmoid(ghy_ref[0].astype(F32)) * b)
    x1 = x_ref[0] + m[:, 2 * D:3 * D] * _dot(mix.astype(BF16), wout_ref[...])
    h2 = _rms_mod(x1, g2_ref[...], m[:, 3 * D:4 * D], m[:, 4 * D:5 * D]).astype(BF16)
    y = None
    for lo, hi in FFN_CHUNKS:
        p = _dot(h2, w1_ref[:, lo:hi])
        act = (p * _sigmoid(p) * _dot(h2, w3_ref[:, lo:hi])).astype(BF16)
        part = _dot(act, w2_ref[lo:hi, :])
        y = part if y is None else y + part
    x2 = x1 + m[:, 5 * D:6 * D] * y
    ms = jnp.mean(x2 * x2, axis=-1, keepdims=True)
    o_ref[0] = x2 * lax.rsqrt(ms + EPS) * gf_ref[...]


def _tail_call(x, yna, yhyt, gna, ghy, mod3, g2, gf, wna, why, wout, w1, w3, w2, nb):
    nt = L // TM
    tok = lambda w: pl.BlockSpec((1, TM, w), lambda b, i: (b, i, 0))
    row = lambda: pl.BlockSpec((1, D), lambda b, i: (0, 0))
    return pl.pallas_call(
        _tail_kernel,
        grid=(nb, nt),
        in_specs=[tok(D), tok(NAW),
                  pl.BlockSpec((1, HYW, TM), lambda b, i: (b, 0, i)),
                  tok(D), tok(D),
                  pl.BlockSpec((1, 1, NMOD * D), lambda b, i: (b, 0, 0)),
                  row(), row(),
                  _const_spec((NAW, D)), _const_spec((HYW, D)), _const_spec((D, D)),
                  _const_spec((D, FFN)), _const_spec((D, FFN)), _const_spec((FFN, D))],
        out_specs=tok(D),
        out_shape=jax.ShapeDtypeStruct((nb, L, D), F32),
        compiler_params=_cp(("arbitrary", "arbitrary")),
        name="tail",
    )(x, yna, yhyt, gna, ghy, mod3, g2, gf, wna, why, wout, w1, w3, w2)


def _rope_tables():
    t = jnp.arange(L)
    rows = (t // GW).astype(F32)
    cols = (t % GW).astype(F32)
    nf = DH // 4
    inv = THETA ** (-jnp.arange(nf, dtype=F32) / nf)
    ar = rows[:, None] * inv
    ac = cols[:, None] * inv
    cos = jnp.concatenate([jnp.cos(ar), jnp.cos(ar), jnp.cos(ac), jnp.cos(ac)], axis=-1)
    sin = jnp.concatenate([-jnp.sin(ar), jnp.sin(ar), -jnp.sin(ac), jnp.sin(ac)], axis=-1)
    return jnp.tile(cos, (1, 2)), jnp.tile(sin, (1, 2))


def _bias_tables(rpb):
    span = QC + KC
    f = jnp.pad(rpb.astype(F32), ((0, 0), (0, 0), (8, span - 8 - 31)), constant_values=NEG)
    b = jnp.tile(f, (1, 1, QC))[:, :, :QC * (span - 1)].reshape(NH, 15, QC, span - 1)
    t = b[..., QC - 1:QC - 1 + KC]
    cq = jnp.arange(QC)[:, None]
    kl = jnp.arange(KC)[None, :]
    ok_a = (kl - cq >= 0) & (kl - cq < 16)
    ok_b = jnp.where(cq < QSHIFT, kl < 16, kl >= 16)
    g = jnp.stack([jnp.where(ok_a, t, NEG), jnp.where(ok_b, t, NEG)], axis=1)
    gx = jnp.pad(g, ((0, 0), (0, 0), (3, 3), (0, 0), (0, 0)), constant_values=NEG)
    return jnp.concatenate([gx[:, :, i:i + NT4] for i in range(4)], axis=-1)


def _dft_tables():
    n2 = jnp.arange(NB)
    ang = (2.0 * math.pi / NB) * ((n2[:, None] * n2[None, :]) % NB).astype(F32)
    fr = jnp.cos(ang)
    fi = -jnp.sin(ang)
    fwd = jnp.concatenate([jnp.concatenate([fr, fi], 1), jnp.concatenate([-fi, fr], 1)], 0)
    inv = jnp.concatenate([jnp.concatenate([fr, -fi], 1), jnp.concatenate([fi, fr], 1)], 0)
    m = jnp.arange(NC)
    tang = (2.0 * math.pi / (2 * NFFT)) * ((n2[None, :] * (4 * m[:, None] + 1)) % (2 * NFFT)).astype(F32)
    twr = jnp.broadcast_to(jnp.cos(tang)[:, None, :], (NC, 8, NB))
    twi = jnp.broadcast_to(-jnp.sin(tang)[:, None, :], (NC, 8, NB))
    return fwd, inv, twr, twi


def _filter_inputs():
    t = jnp.linspace(0.0, 1.0, L, dtype=F32)[:, None]
    w = 2.0 * math.pi * jnp.arange(L, dtype=F32)[:, None] / L
    bands = jnp.linspace(1e-4, HY_BANDS - 1, HY_BANDS, dtype=F32)
    z = jnp.concatenate([t, jnp.cos(bands * w), jnp.sin(-bands * w)], axis=-1)
    zb = jnp.concatenate([z[0:1], z[:0:-1]], axis=0)
    zt = jnp.concatenate([z, zb], axis=0).T
    zt = jnp.pad(zt, ((0, HY_FFN - HY_EMB), (0, 0)))
    tl = t[:, 0]
    tpos = jnp.concatenate([tl, tl[0:1], tl[:0:-1]])[None, :]
    sgn = jnp.concatenate([jnp.ones((L,), F32), jnp.zeros((1,), F32), -jnp.ones((L - 1,), F32)])[None, :]
    min_decay = math.log(1e-2) / 1.5
    max_decay = math.log(1e-2) / 0.3
    deltas = jnp.abs(jnp.linspace(min_decay, max_decay, HYW, dtype=F32))
    return zt, tpos, sgn, deltas


def kernel(x, c, ctx, c_ctx, w_ada, b_ada, norm1_g, norm2_g, w_in, na_rpb, hy_conv_w, hy_conv_b,
           hy_ffn_w1, hy_ffn_b1, hy_ffn_w2, hy_ffn_b2, hy_sin_freq, hy_ffn_w3, hy_bias,
           w_na_o, w_hy_o, w_out, ffn_w1, ffn_w3, ffn_w2, final_g):
    nb = x.shape[0]
    assert x.shape[1:] == (L, D) and w_ada.shape[0] == 1 and nb < 16

    cvec = jnp.zeros((16, D), F32).at[:nb].set(c).at[nb].set(c_ctx)
    mod = _mod_call(cvec, w_ada[0], b_ada[0][None, :])
    mod3 = mod[:, None, :]

    wi = w_in[0]
    g1 = norm1_g[0][None, :]
    kc, vc = _ctx_call(ctx, mod3, g1, wi[:, NAW:3 * NAW].astype(BF16), nb)

    cos, sin = _rope_tables()
    q, qr, kr, v, gna, ghy, u = _inproj_call(x, mod3, g1, wi.astype(BF16), cos, sin, nb)

    yna = _na_call(q, qr, kr, v, kc, vc, _bias_tables(na_rpb[0]), nb)

    fwd, inv, twr, twi = _dft_tables()
    fwd = fwd.astype(BF16)
    inv = inv.astype(BF16)
    zt, tpos, sgn, deltas = _filter_inputs()
    col = lambda a: a.astype(F32)[:, None]
    w1t = jnp.pad(hy_ffn_w1[0].T, ((0, 0), (0, HY_FFN - HY_EMB)))
    hidt = _hymlp_call(zt, w1t, col(hy_ffn_b1[0]), hy_ffn_w2[0].T, col(hy_ffn_b2[0]),
                       col(hy_sin_freq[0]))
    w3t = hy_ffn_w3[0].T
    kf = _filt_call(hidt, w3t[:2 * HYW], w3t[2 * HYW:], jnp.tile(deltas, 2)[:, None],
                    tpos, sgn, twr, twi, fwd)

    bias = jnp.broadcast_to(hy_bias[0][:, :, None], (2, HYW, 128))
    yhyt = _hyena_call(u, hy_conv_w[0], hy_conv_b[0][None, :], kf, bias, twr, twi, fwd, inv, nb)

    return _tail_call(x, yna, yhyt, gna, ghy, mod3, norm2_g[0][None, :], final_g[None, :],
                      w_na_o[0].astype(BF16), w_hy_o[0].astype(BF16), w_out[0].astype(BF16),
                      ffn_w1[0].astype(BF16), ffn_w3[0].astype(BF16), ffn_w2[0].astype(BF16), nb)
```

```python
import cmath
import functools
import math

import jax
import jax.numpy as jnp
from jax import lax
from jax.experimental import pallas as pl
from jax.experimental.pallas import tpu as pltpu

F32 = jnp.float32
BF16 = jnp.bfloat16

D = 1024
L = 4096
GW = 64
CTX = 256
DH = 64
NH = 8
NAW = NH * DH
HYW = 512
FFN = 2816
NMOD = 6
EPS = 1e-6
NEG = -1e30
THETA = 10000.0
HY_EMB = 33
HY_BANDS = 16
HY_FFN = 64

NFFT = 2 * L
NC = 16
NB = 256
TM = 512
CB = 128
QR = 8
BAND = 16
QC = 16
KC = 32
NM = GW // QC
QSHIFT = 8
NT4 = 18
NA_ORDER = "s0 q1 nc v0 nq0 s1 q2 v1 s2 q3 v2 s3 v3"
VMEM_LIMIT = 56 * 1024 * 1024


def _cp(sem):
    return pltpu.CompilerParams(dimension_semantics=sem, vmem_limit_bytes=VMEM_LIMIT)


def _const_spec(shape):
    nd = len(shape)
    return pl.BlockSpec(shape, lambda *_: (0,) * nd, pipeline_mode=pl.Buffered(1))


def _dot(a, b):
    return jnp.dot(a, b, preferred_element_type=F32)


def _dot_nt(a, b):
    return lax.dot_general(a, b, (((1,), (1,)), ((), ())), preferred_element_type=F32)


def _dot_tn(a, b):
    return lax.dot_general(a, b, (((0,), (0,)), ((), ())), preferred_element_type=F32)


def _split(a):
    hi = a.astype(BF16)
    lo = (a - hi.astype(F32)).astype(BF16)
    return hi, lo


def _dot3(a, b, dot=_dot):
    ah, al = _split(a)
    bh, bl = _split(b)
    return dot(ah, bh) + dot(al, bh) + dot(ah, bl)


def _sigmoid(x):
    return 1.0 / (1.0 + jnp.exp(-x))


def _rms_mod(x, g, shift, scale):
    ms = jnp.mean(x * x, axis=-1, keepdims=True)
    y = x * lax.rsqrt(ms + EPS) * g
    return y * (1.0 + scale) + shift


def _mod_kernel(c_ref, w_ref, b_ref, o_ref):
    c = c_ref[...]
    s = c * _sigmoid(c)
    o_ref[...] = _dot3(s, w_ref[...]) + b_ref[...]


def _mod_call(cvec, w_ada, b_ada):
    nt = 4
    tn = NMOD * D // nt
    return pl.pallas_call(
        _mod_kernel,
        grid=(nt,),
        in_specs=[pl.BlockSpec((16, D), lambda i: (0, 0)),
                  pl.BlockSpec((D, tn), lambda i: (0, i)),
                  pl.BlockSpec((1, tn), lambda i: (0, i))],
        out_specs=pl.BlockSpec((16, tn), lambda i: (0, i)),
        out_shape=jax.ShapeDtypeStruct((16, NMOD * D), F32),
        compiler_params=_cp(("arbitrary",)),
        name="mod",
    )(cvec, w_ada, b_ada)


def _ctx_kernel(x_ref, mod_ref, g_ref, w_ref, k_ref, v_ref):
    m = mod_ref[0]
    h = _rms_mod(x_ref[...], g_ref[...], m[:, 0:D], m[:, D:2 * D])
    kv = _dot(h.astype(BF16), w_ref[...])
    k_ref[...] = kv[:, :NAW].astype(BF16)
    v_ref[...] = kv[:, NAW:].astype(BF16)


def _ctx_call(ctx, mod3, g1, w_kv, nb):
    group = max(g for g in (1, 2, 4) if nb % g == 0)
    rb = CTX * group
    rows = lambda w: pl.BlockSpec((rb, w), lambda i: (i, 0))
    kc, vc = pl.pallas_call(
        _ctx_kernel,
        grid=(nb // group,),
        in_specs=[rows(D),
                  pl.BlockSpec((1, 1, NMOD * D), lambda i: (nb, 0, 0)),
                  pl.BlockSpec((1, D), lambda i: (0, 0)),
                  _const_spec((D, 2 * NAW))],
        out_specs=[rows(NAW), rows(NAW)],
        out_shape=[jax.ShapeDtypeStruct((nb * CTX, NAW), BF16)] * 2,
        compiler_params=_cp(("arbitrary",)),
        name="ctx",
    )(ctx.reshape(nb * CTX, D), mod3, g1, w_kv)
    return kc.reshape(nb, CTX, NAW), vc.reshape(nb, CTX, NAW)


def _rope(t, cos, sin_signed, first):
    up = pltpu.roll(t, NAW - 16, axis=1)
    down = pltpu.roll(t, 16, axis=1)
    return t * cos + jnp.where(first, up, down) * sin_signed


def _rot_cols(t):
    t3 = t.reshape(TM // GW, GW, t.shape[-1])
    return jnp.concatenate([t3[:, QSHIFT:], t3[:, :QSHIFT]], axis=1).reshape(t.shape)


def _inproj_kernel(x_ref, mod_ref, g_ref, w_ref, cos_ref, sin_ref,
                   q_ref, qr_ref, kr_ref, v_ref, gna_ref, ghy_ref, u_ref):
    m = mod_ref[0]
    hb = _rms_mod(x_ref[0], g_ref[...], m[:, 0:D], m[:, D:2 * D]).astype(BF16)
    cos = jnp.tile(cos_ref[...], (1, NH // 2))
    sin = jnp.tile(sin_ref[...], (1, NH // 2))
    lane = lax.broadcasted_iota(jnp.int32, (TM, NAW), 1)
    first = (lane & 16) == 0
    q = _dot(hb, w_ref[:, 0:NAW]) * 0.125
    q_ref[0] = _rot_cols(q).astype(BF16)
    qr_ref[0] = _rot_cols(_rope(q, cos, sin, first)).astype(BF16)
    k = _dot(hb, w_ref[:, NAW:2 * NAW])
    kr_ref[0] = _rope(k, cos, sin, first).astype(BF16)
    v_ref[0] = _dot(hb, w_ref[:, 2 * NAW:3 * NAW]).astype(BF16)
    u_ref[0] = _dot(hb, w_ref[:, 3 * NAW:3 * NAW + 3 * HYW])
    g0 = 3 * NAW + 3 * HYW
    gna_ref[0] = _dot(hb, w_ref[:, g0:g0 + D]).astype(BF16)
    ghy_ref[0] = _dot(hb, w_ref[:, g0 + D:g0 + 2 * D]).astype(BF16)


def _inproj_call(x, mod3, g1, w, cos, sin, nb):
    nt = L // TM
    tok = lambda w_: pl.BlockSpec((1, TM, w_), lambda i, b: (b, i, 0))
    return pl.pallas_call(
        _inproj_kernel,
        grid=(nt, nb),
        in_specs=[tok(D),
                  pl.BlockSpec((1, 1, NMOD * D), lambda i, b: (b, 0, 0)),
                  pl.BlockSpec((1, D), lambda i, b: (0, 0)),
                  _const_spec((D, 3 * NAW + 3 * HYW + 2 * D)),
                  pl.BlockSpec((TM, 2 * DH), lambda i, b: (i, 0)),
                  pl.BlockSpec((TM, 2 * DH), lambda i, b: (i, 0))],
        out_specs=[tok(NAW), tok(NAW), tok(NAW), tok(NAW), tok(D), tok(D), tok(3 * HYW)],
        out_shape=[jax.ShapeDtypeStruct((nb, L, NAW), BF16)] * 4
        + [jax.ShapeDtypeStruct((nb, L, D), BF16)] * 2
        + [jax.ShapeDtypeStruct((nb, L, 3 * HYW), F32)],
        compiler_params=_cp(("arbitrary", "arbitrary")),
        name="inproj",
    )(x, mod3, g1, w, cos, sin)


def _na_geometry(r0, start):
    rows = L // GW
    geo = {}
    for ri in range(QR):
        r = r0 + ri
        ws = min(max(r - 4, 0), rows - 8)
        for u in range(BAND // 4):
            ok = [ws <= start + 4 * u + i < ws + 8 for i in range(4)]
            if not any(ok):
                geo[ri, u] = None
                continue
            lo = ok.index(True)
            hi = 4 - ok[::-1].index(True)
            base = start + 4 * u - r + 7
            geo[ri, u] = (base + 3, lo, hi)
    return geo


def _na_block(geo, q0, start_tok, q_ref, qr_ref, k_ref, v_ref, kc_ref, vc_ref, t4_ref, o_ref):
    kc = kc_ref[0]
    vc = vc_ref[0]
    nq = QR * GW
    state = {}

    def ctx_scores():
        q = q_ref[0, pl.ds(q0, nq), :]
        head0 = lax.broadcasted_iota(jnp.int32, (nq, 2 * DH), 1) < DH
        zero = jnp.zeros_like(q)
        q_h = (jnp.where(head0, q, zero), jnp.where(head0, zero, q))
        state["sc"] = _dot_nt(jnp.concatenate(q_h, axis=0), kc)

    half = QR * QC
    head0_t = lax.broadcasted_iota(jnp.int32, (half, 2 * DH), 1) < DH
    zero_t = jnp.zeros((half, 2 * DH), BF16)
    lane_t = lax.broadcasted_iota(jnp.int32, (QC, 2 * DH), 1)
    masks = {}
    for g in geo.values():
        if g is not None and g[1:] != (0, 4) and g[1:] not in masks:
            lo, hi = g[1:]
            masks[lo, hi] = jnp.logical_and(lane_t >= KC * lo, lane_t < KC * hi)
    pzero = jnp.zeros((QC, 2 * DH), BF16)

    def tile_slabs(m):
        return [(GW - QC, QC), (0, QC)] if m == NM - 1 else [(QC * m, KC)]

    def qk(m):
        kt = jnp.concatenate([k_ref[0, pl.ds(start_tok + GW * bi + off, n), :]
                              for bi in range(BAND) for off, n in tile_slabs(m)], axis=0)
        qt = jnp.concatenate([qr_ref[0, pl.ds(q0 + GW * ri + QC * m, QC), :]
                              for ri in range(QR)], axis=0)
        qt = jnp.concatenate([jnp.where(head0_t, qt, zero_t), jnp.where(head0_t, zero_t, qt)],
                             axis=0)
        return _dot_nt(qt, kt)

    def softmax(m, s):
        last = m == NM - 1
        p_rows, pc_rows, dens = [], [], []
        for hh in range(2):
            for ri in range(QR):
                rs = QC * (QR * hh + ri)
                rc = nq * hh + GW * ri + QC * m
                sc_t = state["sc"][rc:rc + QC]
                mvec = jnp.maximum(sc_t[:, :2 * DH], sc_t[:, 2 * DH:])
                tiles = {}
                for u in range(BAND // 4):
                    g = geo[ri, u]
                    if g is None:
                        continue
                    t = s[rs:rs + QC, 2 * DH * u:2 * DH * (u + 1)] + t4_ref[hh, int(last), g[0]]
                    if g[1:] != (0, 4):
                        t = jnp.where(masks[g[1:]], t, NEG)
                    tiles[u] = t
                    mvec = jnp.maximum(mvec, t)
                mx = jnp.max(mvec, axis=-1, keepdims=True)
                pc = jnp.exp(sc_t - mx)
                acc = pc[:, :2 * DH] + pc[:, 2 * DH:]
                prow = []
                for u in range(BAND // 4):
                    if u in tiles:
                        p = jnp.exp(tiles[u] - mx)
                        acc = acc + p
                        prow.append(p.astype(BF16))
                    else:
                        prow.append(pzero)
                p_rows.append(jnp.concatenate(prow, axis=1))
                pc_rows.append(pc.astype(BF16))
                dens.append(jnp.sum(acc, axis=-1, keepdims=True))
        return (jnp.concatenate(p_rows, axis=0), jnp.concatenate(pc_rows, axis=0),
                jnp.concatenate(dens, axis=0))

    def pv(m, p, pc, den):
        vt = jnp.concatenate([v_ref[0, pl.ds(start_tok + GW * bi + off, n), :]
                              for bi in range(BAND) for off, n in tile_slabs(m)], axis=0)
        o = _dot(jnp.concatenate([p, pc], axis=1), jnp.concatenate([vt, vc], axis=0)) / den
        ot = jnp.where(head0_t, o[:half], o[half:])
        for ri in range(QR):
            piece = ot[QC * ri:QC * (ri + 1)]
            if m == NM - 1:
                o_ref[0, pl.ds(q0 + GW * ri + GW - QSHIFT, QSHIFT), :] = piece[:QSHIFT]
                o_ref[0, pl.ds(q0 + GW * ri, QSHIFT), :] = piece[QSHIFT:]
            else:
                o_ref[0, pl.ds(q0 + GW * ri + QC * m + QSHIFT, QC), :] = piece

    def run(stage):
        if stage == "c":
            ctx_scores()
            return
        m = int(stage[1])
        if stage[0] == "q":
            state["s", m] = qk(m)
        elif stage[0] == "s":
            state["p", m] = softmax(m, state.pop(("s", m)))
        else:
            pv(m, *state.pop(("p", m)))

    return run


def _na_kernel(q_ref, qr_ref, k_ref, v_ref, kc_ref, vc_ref, t4_ref, o_ref):
    nq = QR * GW
    nj = L // nq
    rows = L // GW
    args = (q_ref, qr_ref, k_ref, v_ref, kc_ref, vc_ref, t4_ref, o_ref)
    blocks = []
    for j in range(nj):
        r0 = QR * j
        start = min(max(r0 - 4, 0), rows - BAND)
        blocks.append(_na_block(_na_geometry(r0, start), nq * j, GW * start, *args))
    tokens = NA_ORDER.split()
    done = set()

    def issue(j, stage):
        if j < nj and (j, stage) not in done:
            done.add((j, stage))
            blocks[j](stage)

    for tok in tokens:
        if tok.startswith("n"):
            issue(0, tok[1:])
    for j in range(nj):
        for tok in tokens:
            if tok.startswith("n"):
                issue(j + 1, tok[1:])
            else:
                issue(j, tok)


def _na_call(q, qr, kr, v, kc, vc, t4, nb):
    full = lambda n: pl.BlockSpec((1, n, 2 * DH), lambda b, hp: (b, 0, hp))
    return pl.pallas_call(
        _na_kernel,
        grid=(nb, NH // 2),
        in_specs=[full(L), full(L), full(L), full(L), full(CTX), full(CTX),
                  pl.BlockSpec((2, 2, NT4, QC, 2 * DH), lambda b, hp: (hp, 0, 0, 0, 0))],
        out_specs=full(L),
        out_shape=jax.ShapeDtypeStruct((nb, L, NAW), F32),
        compiler_params=_cp(("arbitrary", "arbitrary")),
        name="na",
    )(q, qr, kr, v, kc, vc, t4)


_R2 = math.sqrt(0.5)


def _bfly(e, o, k, n, sign):
    er, ei = e
    orr, oi = o
    if k == 0:
        tr, ti = orr, oi
    elif 4 * k == n:
        if sign < 0:
            return (er + oi, ei - orr), (er - oi, ei + orr)
        return (er - oi, ei + orr), (er + oi, ei - orr)
    elif 8 * k == n:
        if sign < 0:
            tr, ti = (orr + oi) * _R2, (oi - orr) * _R2
        else:
            tr, ti = (orr - oi) * _R2, (oi + orr) * _R2
    elif 8 * k == 3 * n:
        if sign < 0:
            tr, ti = (oi - orr) * _R2, (orr + oi) * (-_R2)
        else:
            tr, ti = (orr + oi) * (-_R2), (orr - oi) * _R2
    else:
        w = cmath.exp(sign * 2j * math.pi * k / n)
        tr = orr * w.real - oi * w.imag
        ti = orr * w.imag + oi * w.real
    return (er + tr, ei + ti), (er - tr, ei - ti)


def _fft(load, n, sign, sink=None, idx=None):
    idx = list(range(n)) if idx is None else idx
    if len(idx) == 1:
        return [load(idx[0])]
    ev = _fft(load, n, sign, None, idx[0::2])
    od = _fft(load, n, sign, None, idx[1::2])
    m = len(idx)
    out = [None] * m
    for k in range(m // 2):
        a, b = _bfly(ev[k], od[k], k, m, sign)
        if sink is None:
            out[k], out[k + m // 2] = a, b
        else:
            sink(k, a)
            sink(k + m // 2, b)
    return out


def _for_row_tiles(row0, nrows, unrolled, body):
    if unrolled:
        for r in range(row0, row0 + nrows, 8):
            body(pl.ds(r, 8))
    else:
        def step(i, carry):
            body(pl.ds(pl.multiple_of(row0 + i * 8, 8), 8))
            return carry
        lax.fori_loop(0, nrows // 8, step, 0)


def _coarse_fwd(x_ref, a_ref, twr_ref, twi_ref, row0, nrows, nblocks, unrolled=False):
    def body(rows):
        for c in range(NB // 128):
            cs = slice(c * 128, (c + 1) * 128)

            def load(n1):
                lo = n1 * NB + c * 128
                x = x_ref[rows, lo:lo + 128]
                w = cmath.exp(-2j * math.pi * n1 / (4 * NC))
                if nblocks == 2 * NC:
                    lo2 = (n1 + NC) * NB + c * 128
                    x2 = x_ref[rows, lo2:lo2 + 128]
                    w2 = cmath.exp(-2j * math.pi * (n1 + NC) / (4 * NC))
                    return (x * w.real + x2 * w2.real, x * w.imag + x2 * w2.imag)
                if n1 == 0:
                    return (x, jnp.zeros_like(x))
                return (x * w.real, x * w.imag)

            def store(m, y):
                tr = twr_ref[m, :, cs]
                ti = twi_ref[m, :, cs]
                ar, ai = y
                a_ref[m, rows, c * 128:(c + 1) * 128] = ar * tr - ai * ti
                a_ref[m, rows, NB + c * 128:NB + (c + 1) * 128] = ar * ti + ai * tr

            _fft(load, NC, -1, store)

    _for_row_tiles(row0, nrows, unrolled, body)


def _coarse_inv(b_ref, twr_ref, twi_ref, z_ref, gate_ref, bias_ref, o_ref, row0, nrows,
                unrolled=False):
    def body(rows):
        bias = bias_ref[rows, :]
        for c in range(NB // 128):
            cs = slice(c * 128, (c + 1) * 128)

            def load(m):
                br = b_ref[m, rows, c * 128:(c + 1) * 128]
                bi = b_ref[m, rows, NB + c * 128:NB + (c + 1) * 128]
                tr = twr_ref[m, :, cs]
                ti = twi_ref[m, :, cs]
                return (br * tr + bi * ti, bi * tr - br * ti)

            def store(n1, yc):
                w = cmath.exp(2j * math.pi * n1 / (4 * NC))
                yr, yi = yc
                y = yr if n1 == 0 else yr * w.real - yi * w.imag
                lo = n1 * NB + c * 128
                z = z_ref[rows, lo:lo + 128]
                o_ref[rows, lo:lo + 128] = gate_ref[rows, lo:lo + 128] * (y + bias * z)

            _fft(load, NC, +1, store)

    _for_row_tiles(row0, nrows, unrolled, body)


def _hymlp_kernel(zt_ref, w1_ref, b1_ref, w2_ref, b2_ref, fr_ref, o_ref):
    fr = fr_ref[...]
    h1 = jnp.sin(fr * (_dot3(w1_ref[...], zt_ref[...]) + b1_ref[...]))
    o_ref[...] = jnp.sin(fr * (_dot3(w2_ref[...], h1) + b2_ref[...]))


def _hymlp_call(zt, w1t, b1, w2t, b2, fr):
    full = lambda s: pl.BlockSpec(s, lambda i: (0,) * len(s))
    return pl.pallas_call(
        _hymlp_kernel,
        grid=(1,),
        in_specs=[full(zt.shape), full(w1t.shape), full(b1.shape),
                  full(w2t.shape), full(b2.shape), full(fr.shape)],
        out_specs=full((HY_FFN, NFFT)),
        out_shape=jax.ShapeDtypeStruct((HY_FFN, NFFT), F32),
        compiler_params=_cp(("arbitrary",)),
        name="hymlp",
    )(zt, w1t, b1, w2t, b2, fr)


def _filt_kernel(h_ref, wf_ref, wb_ref, delta_ref, tpos_ref, sgn_ref, twr_ref, twi_ref,
                 fc_ref, kf_ref, kt_ref, a_ref):
    ht = h_ref[...]
    dec = jnp.exp(-delta_ref[...] * tpos_ref[...]) * sgn_ref[...]
    kt_ref[:, :L] = _dot3(wf_ref[...], ht[:, :L]) * dec[:, :L]
    kt_ref[:, L:] = _dot3(wb_ref[...], ht[:, L:]) * dec[:, L:]
    _coarse_fwd(kt_ref, a_ref, twr_ref, twi_ref, 0, CB, 2 * NC)
    ah, al = _split(a_ref[...].reshape(NC * CB, 2 * NB))
    f = fc_ref[...]
    kf_ref[0] = ((_dot(ah, f) + _dot(al, f)) * (2.0 / NFFT)).reshape(NC, CB, 2 * NB)


def _filt_call(hidt, w3f, w3b, delta, tpos, sgn, twr, twi, fc):
    nblk = 2 * HYW // CB
    per = HYW // CB
    return pl.pallas_call(
        _filt_kernel,
        grid=(nblk,),
        in_specs=[_const_spec((HY_FFN, NFFT)),
                  pl.BlockSpec((CB, HY_FFN), lambda i: (i, 0)),
                  pl.BlockSpec((CB, HY_FFN), lambda i: (i, 0)),
                  pl.BlockSpec((CB, 1), lambda i: (i, 0)),
                  _const_spec((1, NFFT)),
                  _const_spec((1, NFFT)),
                  _const_spec((NC, 8, NB)),
                  _const_spec((NC, 8, NB)),
                  _const_spec((2 * NB, 2 * NB))],
        out_specs=pl.BlockSpec((1, NC, CB, 2 * NB), lambda i: (i // per, 0, i % per, 0)),
        out_shape=jax.ShapeDtypeStruct((2, NC, HYW, 2 * NB), F32),
        scratch_shapes=[pltpu.VMEM((CB, NFFT), F32),
                        pltpu.VMEM((NC, CB, 2 * NB), F32)],
        compiler_params=_cp(("arbitrary",)),
        name="filt",
    )(hidt, w3f, w3b, delta, tpos, sgn, twr, twi, fc)


def _short_conv_t(u_ref, w_ref, b_ref, pad_ref, o_ref, two_pass):
    pad_ref[0:8, :] = jnp.zeros((8, CB), F32)
    pad_ref[L + 8:L + 16, :] = jnp.zeros((8, CB), F32)
    pad_ref[8:L + 8, :] = u_ref[0]
    w = w_ref[...]
    ii = lax.broadcasted_iota(jnp.int32, (CB, CB), 0)
    jj = lax.broadcasted_iota(jnp.int32, (CB, CB), 1)
    eye = jnp.where(ii == jj, 1.0, 0.0).astype(BF16)
    for ch in range(L // TM):
        r = 8 + TM * ch
        s = (pad_ref[r - 1:r - 1 + TM, :] * w[0:1] + pad_ref[r:r + TM, :] * w[1:2]
             + pad_ref[r + 1:r + 1 + TM, :] * w[2:3] + b_ref[...])
        if two_pass:
            hi, lo = _split(s)
            o_ref[:, TM * ch:TM * (ch + 1)] = _dot_nt(eye, hi) + _dot_nt(eye, lo)
        else:
            o_ref[:, TM * ch:TM * (ch + 1)] = _dot_nt(eye, s.astype(BF16))


def _spectral_mul(x_ref, kf_ref, order, y_ref, rows):
    for m in range(NC):
        xr = x_ref[m, rows, :NB]
        xi = x_ref[m, rows, NB:]
        kr = kf_ref[order, m, rows, :NB]
        ki = kf_ref[order, m, rows, NB:]
        y_ref[m, rows, :NB] = xr * kr - xi * ki
        y_ref[m, rows, NB:] = xr * ki + xi * kr


HY_PARTS = 2


def _hyena_order(parts):
    seq = []
    for order in range(2):
        if order == 0:
            for p in range(parts):
                seq += [("fwd", 0, p), ("dft", 0, p)]
        for p in range(parts):
            seq += [("mul", order, p), ("idft", order, p)]
        for p in range(parts):
            seq.append(("inv", order, p))
            if order == 0:
                seq += [("fwd", 1, p), ("dft", 1, p)]
    return seq


def _hyena_kernel(v_ref, x1_ref, x2_ref, cwv_ref, cw1_ref, cw2_ref, cbv_ref, cb1_ref, cb2_ref,
                  kf_ref, bias_ref, twr_ref, twi_ref, fc_ref, fi_ref, o_ref,
                  z_ref, g1_ref, g2_ref, a_ref, x_ref, pad_ref):
    _short_conv_t(v_ref, cwv_ref, cbv_ref, pad_ref, z_ref, True)
    _short_conv_t(x1_ref, cw1_ref, cb1_ref, pad_ref, g1_ref, False)
    _short_conv_t(x2_ref, cw2_ref, cb2_ref, pad_ref, g2_ref, False)
    hb = CB // HY_PARTS

    def matmul(rows, w_ref):
        a2 = a_ref[:, rows, :].reshape(NC * hb, 2 * NB).astype(BF16)
        x_ref[:, rows, :] = _dot(a2, w_ref[...]).reshape(NC, hb, 2 * NB)

    def stage(name, order, half):
        row0 = half * hb
        rows = slice(row0, row0 + hb)
        if name == "fwd":
            _coarse_fwd(z_ref, a_ref, twr_ref, twi_ref, row0, hb, NC, unrolled=True)
        elif name == "dft":
            matmul(rows, fc_ref)
        elif name == "mul":
            _spectral_mul(x_ref, kf_ref, order, a_ref, rows)
        elif name == "idft":
            matmul(rows, fi_ref)
        else:
            gate_ref, dst_ref = ((g1_ref, z_ref), (g2_ref, o_ref.at[0]))[order]
            _coarse_inv(x_ref, twr_ref, twi_ref, z_ref, gate_ref, bias_ref.at[order], dst_ref,
                        row0, hb, unrolled=True)

    for name, order, part in _hyena_order(HY_PARTS):
        stage(name, order, part)


def _hyena_call(u, cw, cb, kf, bias, twr, twi, fc, fi, nb):
    per = HYW // CB
    chan = lambda part: pl.BlockSpec((1, L, CB), lambda c, b: (b, 0, part * per + c))
    roww = lambda n, part: pl.BlockSpec((n, CB), lambda c, b: (0, part * per + c))
    return pl.pallas_call(
        _hyena_kernel,
        grid=(per, nb),
        in_specs=[chan(0), chan(1), chan(2),
                  roww(3, 0), roww(3, 1), roww(3, 2), roww(1, 0), roww(1, 1), roww(1, 2),
                  pl.BlockSpec((2, NC, CB, 2 * NB), lambda c, b: (0, 0, c, 0),
                               pipeline_mode=pl.Buffered(1)),
                  pl.BlockSpec((2, CB, 128), lambda c, b: (0, c, 0)),
                  _const_spec((NC, 8, NB)),
                  _const_spec((NC, 8, NB)),
                  _const_spec((2 * NB, 2 * NB)),
                  _const_spec((2 * NB, 2 * NB))],
        out_specs=pl.BlockSpec((1, CB, L), lambda c, b: (b, c, 0)),
        out_shape=jax.ShapeDtypeStruct((nb, HYW, L), F32),
        scratch_shapes=[pltpu.VMEM((CB, L), F32),
                        pltpu.VMEM((CB, L), F32),
                        pltpu.VMEM((CB, L), F32),
                        pltpu.VMEM((NC, CB, 2 * NB), F32),
                        pltpu.VMEM((NC, CB, 2 * NB), F32),
                        pltpu.VMEM((L + 16, CB), F32)],
        compiler_params=_cp(("arbitrary", "arbitrary")),
        name="hyena",
    )(u, u, u, cw, cw, cw, cb, cb, cb, kf, bias, twr, twi, fc, fi)


FFN_CHUNKS = ((0, 1024), (1024, 2048), (2048, FFN))


def _tail_kernel(x_ref, yna_ref, yhy_ref, gna_ref, ghy_ref, mod_ref, g2_ref, gf_ref,
                 wna_ref, why_ref, wout_ref, w1_ref, w3_ref, w2_ref, o_ref):
    m = mod_ref[0]
    a = _dot(yna_ref[0].astype(BF16), wna_ref[...])
    b = _dot_tn(yhy_ref[0].astype(BF16), why_ref[...])
    mix = (_sigmoid(gna_ref[0].astype(F32)) * a + _sigmoid(ghy_ref[0].astype(F32)) * b)
    x1 = x_ref[0] + m[:, 2 * D:3 * D] * _dot(mix.astype(BF16), wout_ref[...])
    h2 = _rms_mod(x1, g2_ref[...], m[:, 3 * D:4 * D], m[:, 4 * D:5 * D]).astype(BF16)
    y = None
    for lo, hi in FFN_CHUNKS:
        p = _dot(h2, w1_ref[:, lo:hi])
        act = (p * _sigmoid(p) * _dot(h2, w3_ref[:, lo:hi])).astype(BF16)
        part = _dot(act, w2_ref[lo:hi, :])
        y = part if y is None else y + part
    x2 = x1 + m[:, 5 * D:6 * D] * y
    ms = jnp.mean(x2 * x2, axis=-1, keepdims=True)
    o_ref[0] = x2 * lax.rsqrt(ms + EPS) * gf_ref[...]


def _tail_call(x, yna, yhyt, gna, ghy, mod3, g2, gf, wna, why, wout, w1, w3, w2, nb):
    nt = L // TM
    tok = lambda w: pl.BlockSpec((1, TM, w), lambda b, i: (b, i, 0))
    row = lambda: pl.BlockSpec((1, D), lambda b, i: (0, 0))
    return pl.pallas_call(
        _tail_kernel,
        grid=(nb, nt),
        in_specs=[tok(D), tok(NAW),
                  pl.BlockSpec((1, HYW, TM), lambda b, i: (b, 0, i)),
                  tok(D), tok(D),
                  pl.BlockSpec((1, 1, NMOD * D), lambda b, i: (b, 0, 0)),
                  row(), row(),
                  _const_spec((NAW, D)), _const_spec((HYW, D)), _const_spec((D, D)),
                  _const_spec((D, FFN)), _const_spec((D, FFN)), _const_spec((FFN, D))],
        out_specs=tok(D),
        out_shape=jax.ShapeDtypeStruct((nb, L, D), F32),
        compiler_params=_cp(("arbitrary", "arbitrary")),
        name="tail",
    )(x, yna, yhyt, gna, ghy, mod3, g2, gf, wna, why, wout, w1, w3, w2)


def _rope_tables():
    t = jnp.arange(L)
    rows = (t // GW).astype(F32)
    cols = (t % GW).astype(F32)
    nf = DH // 4
    inv = THETA ** (-jnp.arange(nf, dtype=F32) / nf)
    ar = rows[:, None] * inv
    ac = cols[:, None] * inv
    cos = jnp.concatenate([jnp.cos(ar), jnp.cos(ar), jnp.cos(ac), jnp.cos(ac)], axis=-1)
    sin = jnp.concatenate([-jnp.sin(ar), jnp.sin(ar), -jnp.sin(ac), jnp.sin(ac)], axis=-1)
    return jnp.tile(cos, (1, 2)), jnp.tile(sin, (1, 2))


def _bias_tables(rpb):
    span = QC + KC
    f = jnp.pad(rpb.astype(F32), ((0, 0), (0, 0), (8, span - 8 - 31)), constant_values=NEG)
    b = jnp.tile(f, (1, 1, QC))[:, :, :QC * (span - 1)].reshape(NH, 15, QC, span - 1)
    t = b[..., QC - 1:QC - 1 + KC]
    cq = jnp.arange(QC)[:, None]
    kl = jnp.arange(KC)[None, :]
    ok_a = (kl - cq >= 0) & (kl - cq < 16)
    ok_b = jnp.where(cq < QSHIFT, kl < 16, kl >= 16)
    g = jnp.stack([jnp.where(ok_a, t, NEG), jnp.where(ok_b, t, NEG)], axis=1)
    gx = jnp.pad(g, ((0, 0), (0, 0), (3, 3), (0, 0), (0, 0)), constant_values=NEG)
    return jnp.concatenate([gx[:, :, i:i + NT4] for i in range(4)], axis=-1)


def _dft_tables():
    n2 = jnp.arange(NB)
    ang = (2.0 * math.pi / NB) * ((n2[:, None] * n2[None, :]) % NB).astype(F32)
    fr = jnp.cos(ang)
    fi = -jnp.sin(ang)
    fwd = jnp.concatenate([jnp.concatenate([fr, fi], 1), jnp.concatenate([-fi, fr], 1)], 0)
    inv = jnp.concatenate([jnp.concatenate([fr, -fi], 1), jnp.concatenate([fi, fr], 1)], 0)
    m = jnp.arange(NC)
    tang = (2.0 * math.pi / (2 * NFFT)) * ((n2[None, :] * (4 * m[:, None] + 1)) % (2 * NFFT)).astype(F32)
    twr = jnp.broadcast_to(jnp.cos(tang)[:, None, :], (NC, 8, NB))
    twi = jnp.broadcast_to(-jnp.sin(tang)[:, None, :], (NC, 8, NB))
    return fwd, inv, twr, twi


def _filter_inputs():
    t = jnp.linspace(0.0, 1.0, L, dtype=F32)[:, None]
    w = 2.0 * math.pi * jnp.arange(L, dtype=F32)[:, None] / L
    bands = jnp.linspace(1e-4, HY_BANDS - 1, HY_BANDS, dtype=F32)
    z = jnp.concatenate([t, jnp.cos(bands * w), jnp.sin(-bands * w)], axis=-1)
    zb = jnp.concatenate([z[0:1], z[:0:-1]], axis=0)
    zt = jnp.concatenate([z, zb], axis=0).T
    zt = jnp.pad(zt, ((0, HY_FFN - HY_EMB), (0, 0)))
    tl = t[:, 0]
    tpos = jnp.concatenate([tl, tl[0:1], tl[:0:-1]])[None, :]
    sgn = jnp.concatenate([jnp.ones((L,), F32), jnp.zeros((1,), F32), -jnp.ones((L - 1,), F32)])[None, :]
    min_decay = math.log(1e-2) / 1.5
    max_decay = math.log(1e-2) / 0.3
    deltas = jnp.abs(jnp.linspace(min_decay, max_decay, HYW, dtype=F32))
    return zt, tpos, sgn, deltas


def kernel(x, c, ctx, c_ctx, w_ada, b_ada, norm1_g, norm2_g, w_in, na_rpb, hy_conv_w, hy_conv_b,
           hy_ffn_w1, hy_ffn_b1, hy_ffn_w2, hy_ffn_b2, hy_sin_freq, hy_ffn_w3, hy_bias,
           w_na_o, w_hy_o, w_out, ffn_w1, ffn_w3, ffn_w2, final_g):
    nb = x.shape[0]
    assert x.shape[1:] == (L, D) and w_ada.shape[0] == 1 and nb < 16

    cvec = jnp.zeros((16, D), F32).at[:nb].set(c).at[nb].set(c_ctx)
    mod = _mod_call(cvec, w_ada[0], b_ada[0][None, :])
    mod3 = mod[:, None, :]

    wi = w_in[0]
    g1 = norm1_g[0][None, :]
    kc, vc = _ctx_call(ctx, mod3, g1, wi[:, NAW:3 * NAW].astype(BF16), nb)

    cos, sin = _rope_tables()
    q, qr, kr, v, gna, ghy, u = _inproj_call(x, mod3, g1, wi.astype(BF16), cos, sin, nb)

    yna = _na_call(q, qr, kr, v, kc, vc, _bias_tables(na_rpb[0]), nb)

    fwd, inv, twr, twi = _dft_tables()
    fwd = fwd.astype(BF16)
    inv = inv.astype(BF16)
    zt, tpos, sgn, deltas = _filter_inputs()
    col = lambda a: a.astype(F32)[:, None]
    w1t = jnp.pad(hy_ffn_w1[0].T, ((0, 0), (0, HY_FFN - HY_EMB)))
    hidt = _hymlp_call(zt, w1t, col(hy_ffn_b1[0]), hy_ffn_w2[0].T, col(hy_ffn_b2[0]),
                       col(hy_sin_freq[0]))
    w3t = hy_ffn_w3[0].T
    kf = _filt_call(hidt, w3t[:2 * HYW], w3t[2 * HYW:], jnp.tile(deltas, 2)[:, None],
                    tpos, sgn, twr, twi, fwd)

    bias = jnp.broadcast_to(hy_bias[0][:, :, None], (2, HYW, 128))
    yhyt = _hyena_call(u, hy_conv_w[0], hy_conv_b[0][None, :], kf, bias, twr, twi, fwd, inv, nb)

    return _tail_call(x, yna, yhyt, gna, ghy, mod3, norm2_g[0][None, :], final_g[None, :],
                      w_na_o[0].astype(BF16), w_hy_o[0].astype(BF16), w_out[0].astype(BF16),
                      ffn_w1[0].astype(BF16), ffn_w3[0].astype(BF16), ffn_w2[0].astype(BF16), nb)
```

```python
import cmath
import functools
import math

import jax
import jax.numpy as jnp
from jax import lax
from jax.experimental import pallas as pl
from jax.experimental.pallas import tpu as pltpu

F32 = jnp.float32
BF16 = jnp.bfloat16

D = 1024
L = 4096
GW = 64
CTX = 256
DH = 64
NH = 8
NAW = NH * DH
HYW = 512
FFN = 2816
NMOD = 6
EPS = 1e-6
NEG = -1e30
THETA = 10000.0
HY_EMB = 33
HY_BANDS = 16
HY_FFN = 64

NFFT = 2 * L
NC = 16
NB = 256
TM = 512
CB = 128
QR = 8
BAND = 16
QC = 16
KC = 32
NM = GW // QC
QSHIFT = 8
NT4 = 18
NA_ORDER = "s0 q1 nc v0 nq0 s1 q2 v1 s2 q3 v2 s3 v3"
VMEM_LIMIT = 56 * 1024 * 1024


def _cp(sem):
    return pltpu.CompilerParams(dimension_semantics=sem, vmem_limit_bytes=VMEM_LIMIT)


def _const_spec(shape):
    nd = len(shape)
    return pl.BlockSpec(shape, lambda *_: (0,) * nd, pipeline_mode=pl.Buffered(1))


def _dot(a, b):
    return jnp.dot(a, b, preferred_element_type=F32)


def _dot_nt(a, b):
    return lax.dot_general(a, b, (((1,), (1,)), ((), ())), preferred_element_type=F32)


def _dot_tn(a, b):
    return lax.dot_general(a, b, (((0,), (0,)), ((), ())), preferred_element_type=F32)


def _split(a):
    hi = a.astype(BF16)
    lo = (a - hi.astype(F32)).astype(BF16)
    return hi, lo


def _dot3(a, b, dot=_dot):
    ah, al = _split(a)
    bh, bl = _split(b)
    return dot(ah, bh) + dot(al, bh) + dot(ah, bl)


def _sigmoid(x):
    return 1.0 / (1.0 + jnp.exp(-x))


def _rms_mod(x, g, shift, scale):
    ms = jnp.mean(x * x, axis=-1, keepdims=True)
    y = x * lax.rsqrt(ms + EPS) * g
    return y * (1.0 + scale) + shift


def _mod_kernel(c_ref, w_ref, b_ref, o_ref):
    c = c_ref[...]
    s = c * _sigmoid(c)
    o_ref[...] = _dot3(s, w_ref[...]) + b_ref[...]


def _mod_call(cvec, w_ada, b_ada):
    nt = 4
    tn = NMOD * D // nt
    return pl.pallas_call(
        _mod_kernel,
        grid=(nt,),
        in_specs=[pl.BlockSpec((16, D), lambda i: (0, 0)),
                  pl.BlockSpec((D, tn), lambda i: (0, i)),
                  pl.BlockSpec((1, tn), lambda i: (0, i))],
        out_specs=pl.BlockSpec((16, tn), lambda i: (0, i)),
        out_shape=jax.ShapeDtypeStruct((16, NMOD * D), F32),
        compiler_params=_cp(("arbitrary",)),
        name="mod",
    )(cvec, w_ada, b_ada)


def _ctx_kernel(x_ref, mod_ref, g_ref, w_ref, k_ref, v_ref):
    m = mod_ref[0]
    h = _rms_mod(x_ref[...], g_ref[...], m[:, 0:D], m[:, D:2 * D])
    kv = _dot(h.astype(BF16), w_ref[...])
    k_ref[...] = kv[:, :NAW].astype(BF16)
    v_ref[...] = kv[:, NAW:].astype(BF16)


def _ctx_call(ctx, mod3, g1, w_kv, nb):
    group = max(g for g in (1, 2, 4) if nb % g == 0)
    rb = CTX * group
    rows = lambda w: pl.BlockSpec((rb, w), lambda i: (i, 0))
    kc, vc = pl.pallas_call(
        _ctx_kernel,
        grid=(nb // group,),
        in_specs=[rows(D),
                  pl.BlockSpec((1, 1, NMOD * D), lambda i: (nb, 0, 0)),
                  pl.BlockSpec((1, D), lambda i: (0, 0)),
                  _const_spec((D, 2 * NAW))],
        out_specs=[rows(NAW), rows(NAW)],
        out_shape=[jax.ShapeDtypeStruct((nb * CTX, NAW), BF16)] * 2,
        compiler_params=_cp(("arbitrary",)),
        name="ctx",
    )(ctx.reshape(nb * CTX, D), mod3, g1, w_kv)
    return kc.reshape(nb, CTX, NAW), vc.reshape(nb, CTX, NAW)


def _rope(t, cos, sin_signed, first):
    up = pltpu.roll(t, NAW - 16, axis=1)
    down = pltpu.roll(t, 16, axis=1)
    return t * cos + jnp.where(first, up, down) * sin_signed


def _rot_cols(t):
    t3 = t.reshape(TM // GW, GW, t.shape[-1])
    return jnp.concatenate([t3[:, QSHIFT:], t3[:, :QSHIFT]], axis=1).reshape(t.shape)


def _inproj_kernel(x_ref, mod_ref, g_ref, w_ref, cos_ref, sin_ref,
                   q_ref, qr_ref, kr_ref, v_ref, gna_ref, ghy_ref, u_ref):
    m = mod_ref[0]
    hb = _rms_mod(x_ref[0], g_ref[...], m[:, 0:D], m[:, D:2 * D]).astype(BF16)
    cos = jnp.tile(cos_ref[...], (1, NH // 2))
    sin = jnp.tile(sin_ref[...], (1, NH // 2))
    lane = lax.broadcasted_iota(jnp.int32, (TM, NAW), 1)
    first = (lane & 16) == 0
    proj = lambda lo, hi: _dot(hb, w_ref[:, lo:hi].astype(BF16))
    q = proj(0, NAW) * 0.125
    q_ref[0] = _rot_cols(q).astype(BF16)
    qr_ref[0] = _rot_cols(_rope(q, cos, sin, first)).astype(BF16)
    k = proj(NAW, 2 * NAW)
    kr_ref[0] = _rope(k, cos, sin, first).astype(BF16)
    v_ref[0] = proj(2 * NAW, 3 * NAW).astype(BF16)
    u_ref[0] = proj(3 * NAW, 3 * NAW + 3 * HYW)
    g0 = 3 * NAW + 3 * HYW
    gna_ref[0] = proj(g0, g0 + D).astype(BF16)
    ghy_ref[0] = proj(g0 + D, g0 + 2 * D).astype(BF16)


def _inproj_call(x, mod3, g1, w, cos, sin, nb):
    nt = L // TM
    tok = lambda w_: pl.BlockSpec((1, TM, w_), lambda i, b: (b, i, 0))
    return pl.pallas_call(
        _inproj_kernel,
        grid=(nt, nb),
        in_specs=[tok(D),
                  pl.BlockSpec((1, 1, NMOD * D), lambda i, b: (b, 0, 0)),
                  pl.BlockSpec((1, D), lambda i, b: (0, 0)),
                  _const_spec((D, 3 * NAW + 3 * HYW + 2 * D)),
                  pl.BlockSpec((TM, 2 * DH), lambda i, b: (i, 0)),
                  pl.BlockSpec((TM, 2 * DH), lambda i, b: (i, 0))],
        out_specs=[tok(NAW), tok(NAW), tok(NAW), tok(NAW), tok(D), tok(D), tok(3 * HYW)],
        out_shape=[jax.ShapeDtypeStruct((nb, L, NAW), BF16)] * 4
        + [jax.ShapeDtypeStruct((nb, L, D), BF16)] * 2
        + [jax.ShapeDtypeStruct((nb, L, 3 * HYW), F32)],
        compiler_params=_cp(("arbitrary", "arbitrary")),
        name="inproj",
    )(x, mod3, g1, w, cos, sin)


def _na_geometry(r0, start):
    rows = L // GW
    geo = {}
    for ri in range(QR):
        r = r0 + ri
        ws = min(max(r - 4, 0), rows - 8)
        for u in range(BAND // 4):
            ok = [ws <= start + 4 * u + i < ws + 8 for i in range(4)]
            if not any(ok):
                geo[ri, u] = None
                continue
            lo = ok.index(True)
            hi = 4 - ok[::-1].index(True)
            base = start + 4 * u - r + 7
            geo[ri, u] = (base + 3, lo, hi)
    return geo


def _na_block(geo, q0, start_tok, q_ref, qr_ref, k_ref, v_ref, kc_ref, vc_ref, t4_ref, o_ref):
    kc = kc_ref[0]
    vc = vc_ref[0]
    nq = QR * GW
    state = {}

    def ctx_scores():
        q = q_ref[0, pl.ds(q0, nq), :]
        head0 = lax.broadcasted_iota(jnp.int32, (nq, 2 * DH), 1) < DH
        zero = jnp.zeros_like(q)
        q_h = (jnp.where(head0, q, zero), jnp.where(head0, zero, q))
        state["sc"] = _dot_nt(jnp.concatenate(q_h, axis=0), kc)

    half = QR * QC
    head0_t = lax.broadcasted_iota(jnp.int32, (half, 2 * DH), 1) < DH
    zero_t = jnp.zeros((half, 2 * DH), BF16)
    lane_t = lax.broadcasted_iota(jnp.int32, (QC, 2 * DH), 1)
    masks = {}
    for g in geo.values():
        if g is not None and g[1:] != (0, 4) and g[1:] not in masks:
            lo, hi = g[1:]
            masks[lo, hi] = jnp.logical_and(lane_t >= KC * lo, lane_t < KC * hi)
    pzero = jnp.zeros((QC, 2 * DH), BF16)

    def tile_slabs(m):
        return [(GW - QC, QC), (0, QC)] if m == NM - 1 else [(QC * m, KC)]

    def qk(m):
        kt = jnp.concatenate([k_ref[0, pl.ds(start_tok + GW * bi + off, n), :]
                              for bi in range(BAND) for off, n in tile_slabs(m)], axis=0)
        qt = jnp.concatenate([qr_ref[0, pl.ds(q0 + GW * ri + QC * m, QC), :]
                              for ri in range(QR)], axis=0)
        qt = jnp.concatenate([jnp.where(head0_t, qt, zero_t), jnp.where(head0_t, zero_t, qt)],
                             axis=0)
        return _dot_nt(qt, kt)

    def softmax(m, s):
        last = m == NM - 1
        p_rows, pc_rows, dens = [], [], []
        for hh in range(2):
            for ri in range(QR):
                rs = QC * (QR * hh + ri)
                rc = nq * hh + GW * ri + QC * m
                sc_t = state["sc"][rc:rc + QC]
                mvec = jnp.maximum(sc_t[:, :2 * DH], sc_t[:, 2 * DH:])
                tiles = {}
                for u in range(BAND // 4):
                    g = geo[ri, u]
                    if g is None:
                        continue
                    t = s[rs:rs + QC, 2 * DH * u:2 * DH * (u + 1)] + t4_ref[hh, int(last), g[0]]
                    if g[1:] != (0, 4):
                        t = jnp.where(masks[g[1:]], t, NEG)
                    tiles[u] = t
                    mvec = jnp.maximum(mvec, t)
                mx = jnp.max(mvec, axis=-1, keepdims=True)
                pc = jnp.exp(sc_t - mx)
                acc = pc[:, :2 * DH] + pc[:, 2 * DH:]
                prow = []
                for u in range(BAND // 4):
                    if u in tiles:
                        p = jnp.exp(tiles[u] - mx)
                        acc = acc + p
                        prow.append(p.astype(BF16))
                    else:
                        prow.append(pzero)
                p_rows.append(jnp.concatenate(prow, axis=1))
                pc_rows.append(pc.astype(BF16))
                dens.append(jnp.sum(acc, axis=-1, keepdims=True))
        return (jnp.concatenate(p_rows, axis=0), jnp.concatenate(pc_rows, axis=0),
                jnp.concatenate(dens, axis=0))

    def pv(m, p, pc, den):
        vt = jnp.concatenate([v_ref[0, pl.ds(start_tok + GW * bi + off, n), :]
                              for bi in range(BAND) for off, n in tile_slabs(m)], axis=0)
        o = _dot(jnp.concatenate([p, pc], axis=1), jnp.concatenate([vt, vc], axis=0)) / den
        ot = jnp.where(head0_t, o[:half], o[half:])
        for ri in range(QR):
            piece = ot[QC * ri:QC * (ri + 1)]
            if m == NM - 1:
                o_ref[0, pl.ds(q0 + GW * ri + GW - QSHIFT, QSHIFT), :] = piece[:QSHIFT]
                o_ref[0, pl.ds(q0 + GW * ri, QSHIFT), :] = piece[QSHIFT:]
            else:
                o_ref[0, pl.ds(q0 + GW * ri + QC * m + QSHIFT, QC), :] = piece

    def run(stage):
        if stage == "c":
            ctx_scores()
            return
        m = int(stage[1])
        if stage[0] == "q":
            state["s", m] = qk(m)
        elif stage[0] == "s":
            state["p", m] = softmax(m, state.pop(("s", m)))
        else:
            pv(m, *state.pop(("p", m)))

    return run


def _na_kernel(q_ref, qr_ref, k_ref, v_ref, kc_ref, vc_ref, t4_ref, o_ref):
    nq = QR * GW
    nj = L // nq
    rows = L // GW
    args = (q_ref, qr_ref, k_ref, v_ref, kc_ref, vc_ref, t4_ref, o_ref)
    blocks = []
    for j in range(nj):
        r0 = QR * j
        start = min(max(r0 - 4, 0), rows - BAND)
        blocks.append(_na_block(_na_geometry(r0, start), nq * j, GW * start, *args))
    tokens = NA_ORDER.split()
    done = set()

    def issue(j, stage):
        if j < nj and (j, stage) not in done:
            done.add((j, stage))
            blocks[j](stage)

    for tok in tokens:
        if tok.startswith("n"):
            issue(0, tok[1:])
    for j in range(nj):
        for tok in tokens:
            if tok.startswith("n"):
                issue(j + 1, tok[1:])
            else:
                issue(j, tok)


def _na_call(q, qr, kr, v, kc, vc, t4, nb):
    full = lambda n: pl.BlockSpec((1, n, 2 * DH), lambda b, hp: (b, 0, hp))
    return pl.pallas_call(
        _na_kernel,
        grid=(nb, NH // 2),
        in_specs=[full(L), full(L), full(L), full(L), full(CTX), full(CTX),
                  pl.BlockSpec((2, 2, NT4, QC, 2 * DH), lambda b, hp: (hp, 0, 0, 0, 0))],
        out_specs=full(L),
        out_shape=jax.ShapeDtypeStruct((nb, L, NAW), F32),
        compiler_params=_cp(("arbitrary", "arbitrary")),
        name="na",
    )(q, qr, kr, v, kc, vc, t4)


_R2 = math.sqrt(0.5)


def _bfly(e, o, k, n, sign):
    er, ei = e
    orr, oi = o
    if k == 0:
        tr, ti = orr, oi
    elif 4 * k == n:
        if sign < 0:
            return (er + oi, ei - orr), (er - oi, ei + orr)
        return (er - oi, ei + orr), (er + oi, ei - orr)
    elif 8 * k == n:
        if sign < 0:
            tr, ti = (orr + oi) * _R2, (oi - orr) * _R2
        else:
            tr, ti = (orr - oi) * _R2, (oi + orr) * _R2
    elif 8 * k == 3 * n:
        if sign < 0:
            tr, ti = (oi - orr) * _R2, (orr + oi) * (-_R2)
        else:
            tr, ti = (orr + oi) * (-_R2), (orr - oi) * _R2
    else:
        w = cmath.exp(sign * 2j * math.pi * k / n)
        tr = orr * w.real - oi * w.imag
        ti = orr * w.imag + oi * w.real
    return (er + tr, ei + ti), (er - tr, ei - ti)


def _fft(load, n, sign, sink=None, idx=None):
    idx = list(range(n)) if idx is None else idx
    if len(idx) == 1:
        return [load(idx[0])]
    ev = _fft(load, n, sign, None, idx[0::2])
    od = _fft(load, n, sign, None, idx[1::2])
    m = len(idx)
    out = [None] * m
    for k in range(m // 2):
        a, b = _bfly(ev[k], od[k], k, m, sign)
        if sink is None:
            out[k], out[k + m // 2] = a, b
        else:
            sink(k, a)
            sink(k + m // 2, b)
    return out


def _for_row_tiles(row0, nrows, unrolled, body):
    if unrolled:
        for r in range(row0, row0 + nrows, 8):
            body(pl.ds(r, 8))
    else:
        def step(i, carry):
            body(pl.ds(pl.multiple_of(row0 + i * 8, 8), 8))
            return carry
        lax.fori_loop(0, nrows // 8, step, 0)


def _coarse_fwd(x_ref, a_ref, twr_ref, twi_ref, row0, nrows, nblocks, unrolled=False):
    def body(rows):
        for c in range(NB // 128):
            cs = slice(c * 128, (c + 1) * 128)

            def load(n1):
                lo = n1 * NB + c * 128
                x = x_ref[rows, lo:lo + 128]
                w = cmath.exp(-2j * math.pi * n1 / (4 * NC))
                if nblocks == 2 * NC:
                    lo2 = (n1 + NC) * NB + c * 128
                    x2 = x_ref[rows, lo2:lo2 + 128]
                    w2 = cmath.exp(-2j * math.pi * (n1 + NC) / (4 * NC))
                    return (x * w.real + x2 * w2.real, x * w.imag + x2 * w2.imag)
                if n1 == 0:
                    return (x, jnp.zeros_like(x))
                return (x * w.real, x * w.imag)

            def store(m, y):
                tr = twr_ref[m, :, cs]
                ti = twi_ref[m, :, cs]
                ar, ai = y
                a_ref[m, rows, c * 128:(c + 1) * 128] = ar * tr - ai * ti
                a_ref[m, rows, NB + c * 128:NB + (c + 1) * 128] = ar * ti + ai * tr

            _fft(load, NC, -1, store)

    _for_row_tiles(row0, nrows, unrolled, body)


def _coarse_inv(b_ref, twr_ref, twi_ref, z_ref, gate_ref, bias_ref, o_ref, row0, nrows,
                unrolled=False):
    def body(rows):
        bias = bias_ref[rows, :]
        for c in range(NB // 128):
            cs = slice(c * 128, (c + 1) * 128)

            def load(m):
                br = b_ref[m, rows, c * 128:(c + 1) * 128]
                bi = b_ref[m, rows, NB + c * 128:NB + (c + 1) * 128]
                tr = twr_ref[m, :, cs]
                ti = twi_ref[m, :, cs]
                return (br * tr + bi * ti, bi * tr - br * ti)

            def store(n1, yc):
                w = cmath.exp(2j * math.pi * n1 / (4 * NC))
                yr, yi = yc
                y = yr if n1 == 0 else yr * w.real - yi * w.imag
                lo = n1 * NB + c * 128
                z = z_ref[rows, lo:lo + 128]
                o_ref[rows, lo:lo + 128] = gate_ref[rows, lo:lo + 128] * (y + bias * z)

            _fft(load, NC, +1, store)

    _for_row_tiles(row0, nrows, unrolled, body)


def _hymlp_kernel(zt_ref, w1_ref, b1_ref, w2_ref, b2_ref, fr_ref, o_ref):
    fr = fr_ref[...]
    h1 = jnp.sin(fr * (_dot3(w1_ref[...], zt_ref[...]) + b1_ref[...]))
    o_ref[...] = jnp.sin(fr * (_dot3(w2_ref[...], h1) + b2_ref[...]))


def _hymlp_call(zt, w1t, b1, w2t, b2, fr):
    full = lambda s: pl.BlockSpec(s, lambda i: (0,) * len(s))
    return pl.pallas_call(
        _hymlp_kernel,
        grid=(1,),
        in_specs=[full(zt.shape), full(w1t.shape), full(b1.shape),
                  full(w2t.shape), full(b2.shape), full(fr.shape)],
        out_specs=full((HY_FFN, L)),
        out_shape=jax.ShapeDtypeStruct((HY_FFN, L), F32),
        compiler_params=_cp(("arbitrary",)),
        name="hymlp",
    )(zt, w1t, b1, w2t, b2, fr)


def _filt_kernel(h_ref, wf_ref, wb_ref, delta_ref, tpos_ref, sgn_ref, twr_ref, twi_ref,
                 fc_ref, kf_ref, kt_ref, a_ref):
    ht = h_ref[...]
    dec = jnp.exp(-delta_ref[...] * tpos_ref[...]) * sgn_ref[...]
    kt_ref[:, :L] = _dot3(wf_ref[...], ht[:, :L]) * dec[:, :L]
    kt_ref[:, L:] = _dot3(wb_ref[...], ht[:, L:]) * dec[:, L:]
    _coarse_fwd(kt_ref, a_ref, twr_ref, twi_ref, 0, CB, 2 * NC)
    ah, al = _split(a_ref[...].reshape(NC * CB, 2 * NB))
    f = fc_ref[...]
    kf_ref[0] = ((_dot(ah, f) + _dot(al, f)) * (2.0 / NFFT)).reshape(NC, CB, 2 * NB)


def _filt_call(hidt, w3f, w3b, delta, tpos, sgn, twr, twi, fc):
    nblk = 2 * HYW // CB
    per = HYW // CB
    return pl.pallas_call(
        _filt_kernel,
        grid=(nblk,),
        in_specs=[_const_spec((HY_FFN, NFFT)),
                  pl.BlockSpec((CB, HY_FFN), lambda i: (i, 0)),
                  pl.BlockSpec((CB, HY_FFN), lambda i: (i, 0)),
                  pl.BlockSpec((CB, 1), lambda i: (i, 0)),
                  _const_spec((1, NFFT)),
                  _const_spec((1, NFFT)),
                  _const_spec((NC, 8, NB)),
                  _const_spec((NC, 8, NB)),
                  _const_spec((2 * NB, 2 * NB))],
        out_specs=pl.BlockSpec((1, NC, CB, 2 * NB), lambda i: (i // per, 0, i % per, 0)),
        out_shape=jax.ShapeDtypeStruct((2, NC, HYW, 2 * NB), F32),
        scratch_shapes=[pltpu.VMEM((CB, NFFT), F32),
                        pltpu.VMEM((NC, CB, 2 * NB), F32)],
        compiler_params=_cp(("arbitrary",)),
        name="filt",
    )(hidt, w3f, w3b, delta, tpos, sgn, twr, twi, fc)


def _short_conv_t(u_ref, w_ref, b_ref, pad_ref, o_ref, two_pass):
    pad_ref[0:8, :] = jnp.zeros((8, CB), F32)
    pad_ref[L + 8:L + 16, :] = jnp.zeros((8, CB), F32)
    pad_ref[8:L + 8, :] = u_ref[0]
    w = w_ref[...]
    ii = lax.broadcasted_iota(jnp.int32, (CB, CB), 0)
    jj = lax.broadcasted_iota(jnp.int32, (CB, CB), 1)
    eye = jnp.where(ii == jj, 1.0, 0.0).astype(BF16)
    for ch in range(L // TM):
        r = 8 + TM * ch
        s = (pad_ref[r - 1:r - 1 + TM, :] * w[0:1] + pad_ref[r:r + TM, :] * w[1:2]
             + pad_ref[r + 1:r + 1 + TM, :] * w[2:3] + b_ref[...])
        if two_pass:
            hi, lo = _split(s)
            o_ref[:, TM * ch:TM * (ch + 1)] = _dot_nt(eye, hi) + _dot_nt(eye, lo)
        else:
            o_ref[:, TM * ch:TM * (ch + 1)] = _dot_nt(eye, s.astype(BF16))


def _spectral_mul(x_ref, kf_ref, order, y_ref, rows):
    for m in range(NC):
        xr = x_ref[m, rows, :NB]
        xi = x_ref[m, rows, NB:]
        kr = kf_ref[order, m, rows, :NB]
        ki = kf_ref[order, m, rows, NB:]
        y_ref[m, rows, :NB] = xr * kr - xi * ki
        y_ref[m, rows, NB:] = xr * ki + xi * kr


HY_PARTS = 2


def _hyena_order(parts):
    seq = []
    for order in range(2):
        if order == 0:
            for p in range(parts):
                seq += [("fwd", 0, p), ("dft", 0, p)]
        for p in range(parts):
            seq += [("mul", order, p), ("idft", order, p)]
        for p in range(parts):
            seq.append(("inv", order, p))
            if order == 0:
                seq += [("fwd", 1, p), ("dft", 1, p)]
    return seq


def _hyena_kernel(v_ref, x1_ref, x2_ref, cwv_ref, cw1_ref, cw2_ref, cbv_ref, cb1_ref, cb2_ref,
                  kf_ref, bias_ref, twr_ref, twi_ref, fc_ref, fi_ref, o_ref,
                  z_ref, g1_ref, g2_ref, a_ref, x_ref, pad_ref):
    _short_conv_t(v_ref, cwv_ref, cbv_ref, pad_ref, z_ref, True)
    _short_conv_t(x1_ref, cw1_ref, cb1_ref, pad_ref, g1_ref, False)
    _short_conv_t(x2_ref, cw2_ref, cb2_ref, pad_ref, g2_ref, False)
    hb = CB // HY_PARTS

    def matmul(rows, w_ref):
        a2 = a_ref[:, rows, :].reshape(NC * hb, 2 * NB).astype(BF16)
        x_ref[:, rows, :] = _dot(a2, w_ref[...]).reshape(NC, hb, 2 * NB)

    def stage(name, order, half):
        row0 = half * hb
        rows = slice(row0, row0 + hb)
        if name == "fwd":
            _coarse_fwd(z_ref, a_ref, twr_ref, twi_ref, row0, hb, NC, unrolled=True)
        elif name == "dft":
            matmul(rows, fc_ref)
        elif name == "mul":
            _spectral_mul(x_ref, kf_ref, order, a_ref, rows)
        elif name == "idft":
            matmul(rows, fi_ref)
        else:
            gate_ref, dst_ref = ((g1_ref, z_ref), (g2_ref, o_ref.at[0]))[order]
            _coarse_inv(x_ref, twr_ref, twi_ref, z_ref, gate_ref, bias_ref.at[order], dst_ref,
                        row0, hb, unrolled=True)

    for name, order, part in _hyena_order(HY_PARTS):
        stage(name, order, part)


def _hyena_call(u, cw, cb, kf, bias, twr, twi, fc, fi, nb):
    per = HYW // CB
    chan = lambda part: pl.BlockSpec((1, L, CB), lambda c, b: (b, 0, part * per + c))
    roww = lambda n, part: pl.BlockSpec((n, CB), lambda c, b: (0, part * per + c))
    return pl.pallas_call(
        _hyena_kernel,
        grid=(per, nb),
        in_specs=[chan(0), chan(1), chan(2),
                  roww(3, 0), roww(3, 1), roww(3, 2), roww(1, 0), roww(1, 1), roww(1, 2),
                  pl.BlockSpec((2, NC, CB, 2 * NB), lambda c, b: (0, 0, c, 0),
                               pipeline_mode=pl.Buffered(1)),
                  pl.BlockSpec((2, CB, 128), lambda c, b: (0, c, 0)),
                  _const_spec((NC, 8, NB)),
                  _const_spec((NC, 8, NB)),
                  _const_spec((2 * NB, 2 * NB)),
                  _const_spec((2 * NB, 2 * NB))],
        out_specs=pl.BlockSpec((1, CB, L), lambda c, b: (b, c, 0)),
        out_shape=jax.ShapeDtypeStruct((nb, HYW, L), F32),
        scratch_shapes=[pltpu.VMEM((CB, L), F32),
                        pltpu.VMEM((CB, L), F32),
                        pltpu.VMEM((CB, L), F32),
                        pltpu.VMEM((NC, CB, 2 * NB), F32),
                        pltpu.VMEM((NC, CB, 2 * NB), F32),
                        pltpu.VMEM((L + 16, CB), F32)],
        compiler_params=_cp(("arbitrary", "arbitrary")),
        name="hyena",
    )(u, u, u, cw, cw, cw, cb, cb, cb, kf, bias, twr, twi, fc, fi)


FFN_CHUNKS = ((0, 1024), (1024, 2048), (2048, FFN))


def _tail_kernel(x_ref, yna_ref, yhy_ref, gna_ref, ghy_ref, mod_ref, g2_ref, gf_ref,
                 wna_ref, why_ref, wout_ref, w1_ref, w3_ref, w2_ref, o_ref):
    m = mod_ref[0]
    a = _dot(yna_ref[0].astype(BF16), wna_ref[...])
    b = _dot_tn(yhy_ref[0].astype(BF16), why_ref[...])
    mix = (_sigmoid(gna_ref[0].astype(F32)) * a + _sigmoid(ghy_ref[0].astype(F32)) * b)
    x1 = x_ref[0] + m[:, 2 * D:3 * D] * _dot(mix.astype(BF16), wout_ref[...])
    h2 = _rms_mod(x1, g2_ref[...], m[:, 3 * D:4 * D], m[:, 4 * D:5 * D]).astype(BF16)
    y = None
    for lo, hi in FFN_CHUNKS:
        p = _dot(h2, w1_ref[:, lo:hi])
        act = (p * _sigmoid(p) * _dot(h2, w3_ref[:, lo:hi])).astype(BF16)
        part = _dot(act, w2_ref[lo:hi, :])
        y = part if y is None else y + part
    x2 = x1 + m[:, 5 * D:6 * D] * y
    ms = jnp.mean(x2 * x2, axis=-1, keepdims=True)
    o_ref[0] = x2 * lax.rsqrt(ms + EPS) * gf_ref[...]


def _tail_call(x, yna, yhyt, gna, ghy, mod3, g2, gf, wna, why, wout, w1, w3, w2, nb):
    nt = L // TM
    tok = lambda w: pl.BlockSpec((1, TM, w), lambda b, i: (b, i, 0))
    row = lambda: pl.BlockSpec((1, D), lambda b, i: (0, 0))
    return pl.pallas_call(
        _tail_kernel,
        grid=(nb, nt),
        in_specs=[tok(D), tok(NAW),
                  pl.BlockSpec((1, HYW, TM), lambda b, i: (b, 0, i)),
                  tok(D), tok(D),
                  pl.BlockSpec((1, 1, NMOD * D), lambda b, i: (b, 0, 0)),
                  row(), row(),
                  _const_spec((NAW, D)), _const_spec((HYW, D)), _const_spec((D, D)),
                  _const_spec((D, FFN)), _const_spec((D, FFN)), _const_spec((FFN, D))],
        out_specs=tok(D),
        out_shape=jax.ShapeDtypeStruct((nb, L, D), F32),
        compiler_params=_cp(("arbitrary", "arbitrary")),
        name="tail",
    )(x, yna, yhyt, gna, ghy, mod3, g2, gf, wna, why, wout, w1, w3, w2)


def _rope_tables():
    t = jnp.arange(L)
    rows = (t // GW).astype(F32)
    cols = (t % GW).astype(F32)
    nf = DH // 4
    inv = THETA ** (-jnp.arange(nf, dtype=F32) / nf)
    ar = rows[:, None] * inv
    ac = cols[:, None] * inv
    cos = jnp.concatenate([jnp.cos(ar), jnp.cos(ar), jnp.cos(ac), jnp.cos(ac)], axis=-1)
    sin = jnp.concatenate([-jnp.sin(ar), jnp.sin(ar), -jnp.sin(ac), jnp.sin(ac)], axis=-1)
    return jnp.tile(cos, (1, 2)), jnp.tile(sin, (1, 2))


def _bias_tables(rpb):
    span = QC + KC
    f = jnp.pad(rpb.astype(F32), ((0, 0), (0, 0), (8, span - 8 - 31)), constant_values=NEG)
    b = jnp.tile(f, (1, 1, QC))[:, :, :QC * (span - 1)].reshape(NH, 15, QC, span - 1)
    t = b[..., QC - 1:QC - 1 + KC]
    cq = jnp.arange(QC)[:, None]
    kl = jnp.arange(KC)[None, :]
    ok_a = (kl - cq >= 0) & (kl - cq < 16)
    ok_b = jnp.where(cq < QSHIFT, kl < 16, kl >= 16)
    g = jnp.stack([jnp.where(ok_a, t, NEG), jnp.where(ok_b, t, NEG)], axis=1)
    gx = jnp.pad(g, ((0, 0), (0, 0), (3, 3), (0, 0), (0, 0)), constant_values=NEG)
    return jnp.concatenate([gx[:, :, i:i + NT4] for i in range(4)], axis=-1)


def _dft_tables():
    n2 = jnp.arange(NB)
    ang = (2.0 * math.pi / NB) * ((n2[:, None] * n2[None, :]) % NB).astype(F32)
    fr = jnp.cos(ang)
    fi = -jnp.sin(ang)
    fwd = jnp.concatenate([jnp.concatenate([fr, fi], 1), jnp.concatenate([-fi, fr], 1)], 0)
    inv = jnp.concatenate([jnp.concatenate([fr, -fi], 1), jnp.concatenate([fi, fr], 1)], 0)
    m = jnp.arange(NC)
    tang = (2.0 * math.pi / (2 * NFFT)) * ((n2[None, :] * (4 * m[:, None] + 1)) % (2 * NFFT)).astype(F32)
    twr = jnp.broadcast_to(jnp.cos(tang)[:, None, :], (NC, 8, NB))
    twi = jnp.broadcast_to(-jnp.sin(tang)[:, None, :], (NC, 8, NB))
    return fwd, inv, twr, twi


def _filter_inputs():
    t = jnp.linspace(0.0, 1.0, L, dtype=F32)[:, None]
    w = 2.0 * math.pi * jnp.arange(L, dtype=F32)[:, None] / L
    bands = jnp.linspace(1e-4, HY_BANDS - 1, HY_BANDS, dtype=F32)
    z = jnp.concatenate([t, jnp.cos(bands * w), jnp.sin(-bands * w)], axis=-1)
    zt = jnp.pad(z.T, ((0, HY_FFN - HY_EMB), (0, 0)))
    tl = t[:, 0]
    tpos = jnp.concatenate([tl, tl[0:1], tl[:0:-1]])[None, :]
    sgn = jnp.concatenate([jnp.ones((L,), F32), jnp.zeros((1,), F32), -jnp.ones((L - 1,), F32)])[None, :]
    min_decay = math.log(1e-2) / 1.5
    max_decay = math.log(1e-2) / 0.3
    deltas = jnp.abs(jnp.linspace(min_decay, max_decay, HYW, dtype=F32))
    return zt, tpos, sgn, deltas


def kernel(x, c, ctx, c_ctx, w_ada, b_ada, norm1_g, norm2_g, w_in, na_rpb, hy_conv_w, hy_conv_b,
           hy_ffn_w1, hy_ffn_b1, hy_ffn_w2, hy_ffn_b2, hy_sin_freq, hy_ffn_w3, hy_bias,
           w_na_o, w_hy_o, w_out, ffn_w1, ffn_w3, ffn_w2, final_g):
    nb = x.shape[0]
    assert x.shape[1:] == (L, D) and w_ada.shape[0] == 1 and nb < 16

    cvec = jnp.zeros((16, D), F32).at[:nb].set(c).at[nb].set(c_ctx)
    mod = _mod_call(cvec, w_ada[0], b_ada[0][None, :])
    mod3 = mod[:, None, :]

    wi = w_in[0]
    g1 = norm1_g[0][None, :]
    kc, vc = _ctx_call(ctx, mod3, g1, wi[:, NAW:3 * NAW].astype(BF16), nb)

    cos, sin = _rope_tables()
    q, qr, kr, v, gna, ghy, u = _inproj_call(x, mod3, g1, wi, cos, sin, nb)

    yna = _na_call(q, qr, kr, v, kc, vc, _bias_tables(na_rpb[0]), nb)

    fwd, inv, twr, twi = _dft_tables()
    fwd = fwd.astype(BF16)
    inv = inv.astype(BF16)
    zt, tpos, sgn, deltas = _filter_inputs()
    col = lambda a: a.astype(F32)[:, None]
    w1t = jnp.pad(hy_ffn_w1[0].T, ((0, 0), (0, HY_FFN - HY_EMB)))
    hid = _hymlp_call(zt, w1t, col(hy_ffn_b1[0]), hy_ffn_w2[0].T, col(hy_ffn_b2[0]),
                      col(hy_sin_freq[0]))
    hidt = jnp.concatenate([hid, hid[:, 0:1], hid[:, :0:-1]], axis=1)
    w3t = hy_ffn_w3[0].T
    kf = _filt_call(hidt, w3t[:2 * HYW], w3t[2 * HYW:], jnp.tile(deltas, 2)[:, None],
                    tpos, sgn, twr, twi, fwd)

    bias = jnp.broadcast_to(hy_bias[0][:, :, None], (2, HYW, 128))
    yhyt = _hyena_call(u, hy_conv_w[0], hy_conv_b[0][None, :], kf, bias, twr, twi, fwd, inv, nb)

    return _tail_call(x, yna, yhyt, gna, ghy, mod3, norm2_g[0][None, :], final_g[None, :],
                      w_na_o[0].astype(BF16), w_hy_o[0].astype(BF16), w_out[0].astype(BF16),
                      ffn_w1[0].astype(BF16), ffn_w3[0].astype(BF16), ffn_w2[0].astype(BF16), nb)
```
